```python
import jax, jax.numpy as jnp
from jax import lax
import numpy as np

D_MODEL = 2048
BATCH = 4
SEQ = 8192
DEPTH = 1

CHUNK = 64
Q_BLOCK = 128
EPS = 1e-6
D_RNN = 2048
RNN_BLOCKS = 16
RNN_BLOCK_DIM = D_RNN // RNN_BLOCKS
CONV_WIDTH = 4
LRU_C = 8.0
N_HEADS = 16
Q_LORA = 512
KV_LORA = 256
QK_NOPE = 128
QK_ROPE = 64
QK_HEAD = QK_NOPE + QK_ROPE
V_HEAD = 128
ROPE_THETA = 10000.0
N_BRANCH = 2
IN_WIDTHS = (D_RNN, D_RNN, Q_LORA, KV_LORA, QK_ROPE, N_BRANCH * D_MODEL)
D_IN = 2 * D_RNN + Q_LORA + KV_LORA + QK_ROPE + N_BRANCH * D_MODEL
N_EXPERTS = 64
TOP_K = 6
D_EXPERT = 1408
D_SHARED = 1408
ROUTED_SCALE = 2.5
MOE_BLOCK = 256
MOD_SCALE = 0.25

kernel_name = 'hybrid_rglru_mla_moe_adaln_block'


def rmsnorm(x, g):
    xf = x.astype(jnp.float32)
    y = xf * lax.rsqrt(jnp.mean(xf * xf, axis=-1, keepdims=True) + EPS)
    return (y * g.astype(jnp.float32)).astype(x.dtype)


def modulate(x, g, shift, scale):
    return rmsnorm(x, g) * (1 + scale[:, None, :]) + shift[:, None, :]


def apply_rope(x, cos, sin):
    x1, x2 = jnp.split(x, 2, axis=-1)
    return jnp.concatenate([x1 * cos - x2 * sin, x2 * cos + x1 * sin], axis=-1)


def swiglu(x, wg, wu, wd):
    return (jax.nn.silu(x @ wg) * (x @ wu)) @ wd


def rglru_branch(xr, yg, conv_w, conv_b, w_a, b_a, w_i, b_i, lam):
    B_, S, _ = xr.shape
    xp = jnp.pad(xr, ((0, 0), (CONV_WIDTH - 1, 0), (0, 0)))
    xc = conv_b + xp[:, 0:S, :] * conv_w[0]
    for j in range(1, CONV_WIDTH):
        xc = xc + xp[:, j:j + S, :] * conv_w[j]
    xb = xc.reshape(B_, S, RNN_BLOCKS, RNN_BLOCK_DIM)
    r = jax.nn.sigmoid(jnp.einsum('bshi,hij->bshj', xb, w_a).reshape(B_, S, D_RNN) + b_a)
    i = jax.nn.sigmoid(jnp.einsum('bshi,hij->bshj', xb, w_i).reshape(B_, S, D_RNN) + b_i)
    log_a = -LRU_C * r.astype(jnp.float32) * jax.nn.softplus(-lam.astype(jnp.float32))
    a = jnp.exp(log_a)
    b = jnp.sqrt(-jnp.expm1(2.0 * log_a)) * (i * xc).astype(jnp.float32)

    def step(h, ab):
        a_t, b_t = ab
        h = a_t * h + b_t
        return h, h

    _, hs = lax.scan(step, jnp.zeros((B_, D_RNN), jnp.float32),
                     (a.transpose(1, 0, 2), b.transpose(1, 0, 2)))
    h = hs.transpose(1, 0, 2).astype(xr.dtype)
    return h * jax.nn.gelu(yg)


def mla_branch(c_q, c_kv, k_rope, positions, q_a_norm, kv_a_norm, w_uq, w_ukv, q_norm, k_norm):
    B_, S, _ = c_q.shape
    q = jnp.einsum('bsr,rhd->bshd', rmsnorm(c_q, q_a_norm), w_uq)
    kv = jnp.einsum('bsr,rhd->bshd', rmsnorm(c_kv, kv_a_norm), w_ukv)
    k_nope, v = kv[..., :QK_NOPE], kv[..., QK_NOPE:]
    k = jnp.concatenate([k_nope, jnp.broadcast_to(k_rope[:, :, None, :], (B_, S, N_HEADS, QK_ROPE))], axis=-1)
    q = rmsnorm(q, q_norm)
    k = rmsnorm(k, k_norm)
    inv_freq = ROPE_THETA ** (-jnp.arange(0, QK_ROPE, 2, dtype=jnp.float32) / QK_ROPE)
    ang = positions.astype(jnp.float32)[..., None] * inv_freq
    cos = jnp.cos(ang)[:, :, None, :].astype(q.dtype)
    sin = jnp.sin(ang)[:, :, None, :].astype(q.dtype)
    q = jnp.concatenate([q[..., :QK_NOPE], apply_rope(q[..., QK_NOPE:], cos, sin)], axis=-1)
    k = jnp.concatenate([k[..., :QK_NOPE], apply_rope(k[..., QK_NOPE:], cos, sin)], axis=-1)
    nqb = S // Q_BLOCK
    qb = q.reshape(B_, nqb, Q_BLOCK, N_HEADS, QK_HEAD).transpose(1, 0, 2, 3, 4)
    key_chunk = jnp.arange(S) // CHUNK
    scale = QK_HEAD ** -0.5

    def attend(args):
        q_blk, blk = args
        q_chunk = (blk * Q_BLOCK + jnp.arange(Q_BLOCK)) // CHUNK
        s = jnp.einsum('bqhd,bkhd->bhqk', q_blk, k).astype(jnp.float32) * scale
        mask = key_chunk[None, :] <= q_chunk[:, None]
        s = jnp.where(mask, s, -jnp.inf)
        p = jax.nn.softmax(s, axis=-1).astype(v.dtype)
        return jnp.einsum('bhqk,bkhd->bqhd', p, v)

    o = lax.map(attend, (qb, jnp.arange(nqb)))
    return o.transpose(1, 0, 2, 3, 4).reshape(B_, S, N_HEADS * V_HEAD)


def moe(u, w_router, router_bias, w_gate, w_up, w_down, ws_gate, ws_up, ws_down):
    B_, S, D = u.shape
    T = B_ * S
    xf = u.reshape(T, D)
    scores = jax.nn.sigmoid(xf.astype(jnp.float32) @ w_router.astype(jnp.float32))
    _, top_idx = lax.top_k(scores + router_bias.astype(jnp.float32), TOP_K)
    top_s = jnp.take_along_axis(scores, top_idx, axis=-1)
    top_w = top_s / jnp.sum(top_s, axis=-1, keepdims=True) * ROUTED_SCALE
    TK = T * TOP_K
    e_flat = top_idx.reshape(TK)
    tok_flat = jnp.arange(TK, dtype=jnp.int32) // TOP_K
    w_flat = top_w.reshape(TK)
    order = jnp.argsort(e_flat)
    sorted_e = e_flat[order]
    sizes = jnp.bincount(e_flat, length=N_EXPERTS)
    padded = (sizes + MOE_BLOCK - 1) // MOE_BLOCK * MOE_BLOCK
    pad_end = jnp.cumsum(padded)
    pad_start = pad_end - padded
    grp_start = jnp.cumsum(sizes) - sizes
    dest = pad_start[sorted_e] + jnp.arange(TK) - grp_start[sorted_e]
    nb = -(-TK // MOE_BLOCK) + N_EXPERTS
    row_tok = jnp.full((nb * MOE_BLOCK,), T, jnp.int32).at[dest].set(tok_flat[order])
    row_w = jnp.zeros((nb * MOE_BLOCK,), jnp.float32).at[dest].set(w_flat[order])
    blk_e = jnp.minimum(jnp.searchsorted(pad_end, jnp.arange(nb) * MOE_BLOCK, side='right'), N_EXPERTS - 1)
    x_pad = jnp.concatenate([xf, jnp.zeros((1, D), xf.dtype)], axis=0)

    def step(acc, blk):
        rows, wts, e = blk
        xb = x_pad[rows]
        y = swiglu(xb, w_gate[e], w_up[e], w_down[e]) * wts[:, None].astype(xb.dtype)
        return acc.at[rows].add(y), None

    acc, _ = lax.scan(step, jnp.zeros_like(x_pad),
                      (row_tok.reshape(nb, MOE_BLOCK), row_w.reshape(nb, MOE_BLOCK), blk_e))
    out = acc[:T] + swiglu(xf, ws_gate, ws_up, ws_down)
    return out.reshape(B_, S, D)


def setup_inputs(seed: int = 0) -> dict:
    key = jax.random.key(seed)
    ks = iter(jax.random.split(key, 40))
    f32 = jnp.float32

    def nrm(shape, scale):
        return jax.random.normal(next(ks), shape, f32) * scale

    def gain(shape):
        return 1.0 + 0.02 * jax.random.normal(next(ks), shape, f32)

    L = DEPTH
    x = jax.random.normal(next(ks), (BATCH, SEQ, D_MODEL), f32)
    c = jax.random.normal(next(ks), (BATCH, D_MODEL), f32)
    positions = (jnp.arange(SEQ, dtype=jnp.int32)[None, :]
                 + jax.random.randint(next(ks), (BATCH, 1), 0, 4096, dtype=jnp.int32))
    a0 = jax.random.uniform(next(ks), (L, D_RNN), f32, 0.9, 0.999) ** (1.0 / LRU_C)
    lru_lambda = jnp.log(a0) - jnp.log1p(-a0)
    return {
        'x': x,
        'c': c,
        'positions': positions,
        'w_mod': nrm((L, D_MODEL, 6 * D_MODEL), MOD_SCALE * D_MODEL ** -0.5),
        'b_mod': nrm((L, 6 * D_MODEL), 0.02),
        'norm1': gain((L, D_MODEL)),
        'w_in': nrm((L, D_MODEL, D_IN), D_MODEL ** -0.5),
        'conv_w': nrm((L, CONV_WIDTH, D_RNN), CONV_WIDTH ** -0.5),
        'conv_b': nrm((L, D_RNN), 0.02),
        'w_a': nrm((L, RNN_BLOCKS, RNN_BLOCK_DIM, RNN_BLOCK_DIM), RNN_BLOCK_DIM ** -0.5),
        'b_a': nrm((L, D_RNN), 0.02),
        'w_i': nrm((L, RNN_BLOCKS, RNN_BLOCK_DIM, RNN_BLOCK_DIM), RNN_BLOCK_DIM ** -0.5),
        'b_i': nrm((L, D_RNN), 0.02),
        'lru_lambda': lru_lambda,
        'q_a_norm': gain((L, Q_LORA)),
        'kv_a_norm': gain((L, KV_LORA)),
        'w_uq': nrm((L, Q_LORA, N_HEADS, QK_HEAD), Q_LORA ** -0.5),
        'w_ukv': nrm((L, KV_LORA, N_HEADS, QK_NOPE + V_HEAD), KV_LORA ** -0.5),
        'q_norm': gain((L, QK_HEAD)),
        'k_norm': gain((L, QK_HEAD)),
        'w_rnn_out': nrm((L, D_RNN, D_MODEL), D_RNN ** -0.5),
        'w_mla_out': nrm((L, N_HEADS * V_HEAD, D_MODEL), (N_HEADS * V_HEAD) ** -0.5),
        'w_out': nrm((L, D_MODEL, D_MODEL), D_MODEL ** -0.5),
        'norm2': gain((L, D_MODEL)),
        'w_router': nrm((L, D_MODEL, N_EXPERTS), D_MODEL ** -0.5),
        'router_bias': nrm((L, N_EXPERTS), 0.01),
        'w_gate': nrm((L, N_EXPERTS, D_MODEL, D_EXPERT), D_MODEL ** -0.5),
        'w_up': nrm((L, N_EXPERTS, D_MODEL, D_EXPERT), D_MODEL ** -0.5),
        'w_down': nrm((L, N_EXPERTS, D_EXPERT, D_MODEL), D_EXPERT ** -0.5),
        'ws_gate': nrm((L, D_MODEL, D_SHARED), D_MODEL ** -0.5),
        'ws_up': nrm((L, D_MODEL, D_SHARED), D_MODEL ** -0.5),
        'ws_down': nrm((L, D_SHARED, D_MODEL), D_SHARED ** -0.5),
    }


def reference(x, c, positions, w_mod, b_mod, norm1, w_in, conv_w, conv_b, w_a, b_a, w_i, b_i,
              lru_lambda, q_a_norm, kv_a_norm, w_uq, w_ukv, q_norm, k_norm, w_rnn_out, w_mla_out,
              w_out, norm2, w_router, router_bias, w_gate, w_up, w_down, ws_gate, ws_up, ws_down):
    split_at = [int(s) for s in np.cumsum(IN_WIDTHS)[:-1]]
    for l in range(DEPTH):
        mod = jax.nn.silu(c) @ w_mod[l] + b_mod[l]
        sh1, sc1, g1, sh2, sc2, g2 = jnp.split(mod, 6, axis=-1)
        u = modulate(x, norm1[l], sh1, sc1)
        proj = u @ w_in[l]
        xr, yg, cq, ckv, kr, gates = jnp.split(proj, split_at, axis=-1)
        y_rnn = rglru_branch(xr, yg, conv_w[l], conv_b[l], w_a[l], b_a[l], w_i[l], b_i[l], lru_lambda[l])
        y_mla = mla_branch(cq, ckv, kr, positions, q_a_norm[l], kv_a_norm[l], w_uq[l], w_ukv[l],
                           q_norm[l], k_norm[l])
        g_rnn, g_mla = jnp.split(jax.nn.sigmoid(gates), N_BRANCH, axis=-1)
        merged = g_rnn * (y_rnn @ w_rnn_out[l]) + g_mla * (y_mla @ w_mla_out[l])
        x = x + g1[:, None, :] * (merged @ w_out[l])
        v = modulate(x, norm2[l], sh2, sc2)
        ffn = moe(v, w_router[l], router_bias[l], w_gate[l], w_up[l], w_down[l],
                  ws_gate[l], ws_up[l], ws_down[l])
        x = x + g2[:, None, :] * ffn
    return x
```

```python
import functools

import jax
import jax.numpy as jnp
from jax import lax
from jax.experimental import pallas as pl
from jax.experimental.pallas import tpu as pltpu

F32 = jnp.float32
BF16 = jnp.bfloat16

EPS = 1e-6
CHUNK = 64
LRU_C = 8.0
ROPE_THETA = 10000.0
TOP_K = 6
ROUTED_SCALE = 2.5
RNN_BLOCK_DIM = 128

LANES = 128
SUBLANES = 8
VMEM_LIMIT_BYTES = 56 * 1024 * 1024

MOE_BLOCK = 256
DEST_SLOTS = 8


def _cparams(semantics):
    return pltpu.CompilerParams(dimension_semantics=semantics, vmem_limit_bytes=VMEM_LIMIT_BYTES)


def _tile(n, target, align=LANES):
    if n <= target:
        return n
    t = (target // align) * align
    while t > align and n % t:
        t -= align
    assert n % t == 0, (n, target, align)
    return t


def _mod_kernel(c_ref, w_ref, b_ref, o_ref):
    c = c_ref[...]
    s = c * jax.nn.sigmoid(c)
    o_ref[...] = jnp.dot(s, w_ref[...], preferred_element_type=F32,
                         precision=lax.Precision.HIGHEST) + b_ref[...]


def adaln_mod(c, w_mod, b_mod):
    B, D = c.shape
    N = w_mod.shape[1]
    rows = -(-B // SUBLANES) * SUBLANES
    c_pad = jnp.zeros((rows, D), F32).at[:B].set(c)
    tn = _tile(N, 1024)
    out = pl.pallas_call(
        _mod_kernel,
        grid=(N // tn,),
        in_specs=[pl.BlockSpec((rows, D), lambda j: (0, 0)),
                  pl.BlockSpec((D, tn), lambda j: (0, j)),
                  pl.BlockSpec((1, tn), lambda j: (0, j))],
        out_specs=pl.BlockSpec((rows, tn), lambda j: (0, j)),
        out_shape=jax.ShapeDtypeStruct((rows, N), F32),
        compiler_params=_cparams(("arbitrary",)),
        name="adaln_mod",
    )(c_pad, w_mod, b_mod.reshape(1, N))
    return out[:B].reshape(B, N // D, D)


def _inproj_kernel(x_ref, mod_ref, g_ref, w_ref, o_ref, u_ref):
    @pl.when(pl.program_id(1) == 0)
    def _():
        x = x_ref[...]
        ms = jnp.mean(x * x, axis=-1, keepdims=True)
        y = x * lax.rsqrt(ms + EPS) * g_ref[...]
        u_ref[...] = (y * (1.0 + mod_ref[0, 1:2, :]) + mod_ref[0, 0:1, :]).astype(BF16)

    o_ref[...] = jnp.dot(u_ref[...], w_ref[...], preferred_element_type=F32).astype(o_ref.dtype)


def in_projection(x2, mod, norm1, w_cat, seq, tm, tn):
    T, D = x2.shape
    N = w_cat.shape[1]
    per_b = seq // tm
    return pl.pallas_call(
        _inproj_kernel,
        grid=(T // tm, N // tn),
        in_specs=[pl.BlockSpec((tm, D), lambda i, j: (i, 0)),
                  pl.BlockSpec((1,) + mod.shape[1:], lambda i, j: (i // per_b, 0, 0)),
                  pl.BlockSpec((1, D), lambda i, j: (0, 0)),
                  pl.BlockSpec((D, tn), lambda i, j: (0, j))],
        out_specs=pl.BlockSpec((tm, tn), lambda i, j: (i, j)),
        out_shape=jax.ShapeDtypeStruct((T, N), BF16),
        scratch_shapes=[pltpu.VMEM((tm, D), BF16)],
        compiler_params=_cparams(("arbitrary", "arbitrary")),
        name="in_projection",
    )(x2, mod, norm1.reshape(1, D), w_cat)


def _gelu_tanh(x):
    return 0.5 * x * (1.0 + jnp.tanh(0.7978845608028654 * (x + 0.044715 * (x * x * x))))


def _rglru_kernel(xr_ref, yg_ref, cw_ref, cb_ref, wai_ref, ba_ref, bi_ref, lam_ref, o_ref,
                  xbuf, xcbuf, abuf, bbuf, hc, *, conv_width):
    ts, dr = xcbuf.shape
    halo = SUBLANES
    nblk = dr // RNN_BLOCK_DIM

    @pl.when(pl.program_id(1) == 0)
    def _():
        xbuf[0:halo, :] = jnp.zeros((halo, dr), F32)
        hc[...] = jnp.zeros_like(hc)

    xbuf[halo:halo + ts, :] = xr_ref[...].astype(F32)
    xc = cb_ref[...] + xbuf[halo:halo + ts, :] * cw_ref[conv_width - 1:conv_width, :]
    for j in range(conv_width - 1):
        off = halo - (conv_width - 1) + j
        xc = xc + xbuf[off:off + ts, :] * cw_ref[j:j + 1, :]
    xcbuf[...] = xc
    xbuf[0:halo, :] = xbuf[ts:ts + halo, :]

    z = -lam_ref[...]
    sp = jnp.maximum(z, 0.0) + jnp.log1p(jnp.exp(-jnp.abs(z)))
    for k in range(nblk):
        cols = slice(k * RNN_BLOCK_DIM, (k + 1) * RNN_BLOCK_DIM)
        xk = xcbuf[:, cols]
        g = jnp.dot(xk.astype(BF16), wai_ref[k], preferred_element_type=F32)
        r = jax.nn.sigmoid(g[:, :RNN_BLOCK_DIM] + ba_ref[:, cols])
        ig = jax.nn.sigmoid(g[:, RNN_BLOCK_DIM:] + bi_ref[:, cols])
        a = jnp.exp((-LRU_C) * r * sp[:, cols])
        abuf[:, cols] = a
        bbuf[:, cols] = jnp.sqrt(1.0 - a * a) * (ig * xk)

    row = lax.broadcasted_iota(jnp.int32, (SUBLANES, dr), 0)
    rows2 = 2 * SUBLANES

    def scan8(a, b, h):
        for s in (1, 2, 4):
            m = row >= s
            a_s = jnp.where(m, pltpu.roll(a, s, 0), 1.0)
            b_s = jnp.where(m, pltpu.roll(b, s, 0), 0.0)
            b = a * b_s + b
            a = a * a_s
        hh = a * h + b
        return hh, jnp.broadcast_to(hh[SUBLANES - 1:SUBLANES, :], (SUBLANES, dr))

    def body(c, h):
        r0 = pl.multiple_of(c * rows2, rows2)
        a16 = abuf[pl.ds(r0, rows2), :]
        b16 = bbuf[pl.ds(r0, rows2), :]
        h0, h = scan8(a16[:SUBLANES], b16[:SUBLANES], h)
        h1, h = scan8(a16[SUBLANES:], b16[SUBLANES:], h)
        hs = jnp.concatenate([h0, h1], axis=0)
        yg = yg_ref[pl.ds(r0, rows2), :].astype(F32)
        o_ref[pl.ds(r0, rows2), :] = (hs * _gelu_tanh(yg)).astype(o_ref.dtype)
        return h

    hc[...] = lax.fori_loop(0, ts // rows2, body, hc[...])


def rglru_branch(proj, conv_w, conv_b, wai, b_a, b_i, lam, batch, seq, ts):
    T = proj.shape[0]
    W, dr = conv_w.shape
    nblk = dr // RNN_BLOCK_DIM
    per_b = seq // ts
    vec = lambda: pl.BlockSpec((1, dr), lambda b, i: (0, 0))
    return pl.pallas_call(
        functools.partial(_rglru_kernel, conv_width=W),
        grid=(batch, per_b),
        in_specs=[pl.BlockSpec((ts, dr), lambda b, i: (b * per_b + i, 0)),
                  pl.BlockSpec((ts, dr), lambda b, i: (b * per_b + i, 1)),
                  pl.BlockSpec((W, dr), lambda b, i: (0, 0)),
                  vec(),
                  pl.BlockSpec((nblk, RNN_BLOCK_DIM, 2 * RNN_BLOCK_DIM), lambda b, i: (0, 0, 0)),
                  vec(), vec(), vec()],
        out_specs=pl.BlockSpec((ts, dr), lambda b, i: (b * per_b + i, 0)),
        out_shape=jax.ShapeDtypeStruct((T, dr), BF16),
        scratch_shapes=[pltpu.VMEM((ts + SUBLANES, dr), F32),
                        pltpu.VMEM((ts, dr), F32),
                        pltpu.VMEM((ts, dr), F32),
                        pltpu.VMEM((ts, dr), F32),
                        pltpu.VMEM((SUBLANES, dr), F32)],
        compiler_params=_cparams(("arbitrary", "arbitrary")),
        name="rglru_branch",
    )(proj, proj, conv_w, conv_b.reshape(1, dr), wai, b_a.reshape(1, dr), b_i.reshape(1, dr),
      lam.reshape(1, dr))


QK_NOPE = 128
QK_ROPE = 64
V_HEAD = 128
QK_HEAD = QK_NOPE + QK_ROPE


def _rot_half(w):
    half = QK_ROPE // 2
    return jnp.concatenate([-w[..., half:], w[..., :half]], axis=-1)


def mla_weights(w_uq, w_ukv, q_norm, k_norm):
    half = QK_ROPE // 2
    rope_w = w_uq[..., QK_NOPE:]
    wq = jnp.concatenate([w_uq[..., :QK_NOPE], rope_w, _rot_half(rope_w)], axis=-1)
    wq = wq.transpose(1, 0, 2).astype(BF16)
    wkv = w_ukv.transpose(1, 0, 2).astype(BF16)

    def gains(g):
        gr = g[QK_NOPE:]
        return jnp.concatenate([g[:QK_NOPE], gr, gr[half:], gr[:half]]).reshape(1, -1).astype(F32)

    inv_freq = ROPE_THETA ** (-jnp.arange(0, QK_ROPE, 2, dtype=F32) / QK_ROPE)
    freq = jnp.tile(inv_freq, 2 * LANES // QK_ROPE).reshape(1, LANES)
    return wq, wkv, gains(q_norm), gains(k_norm), freq


def _mla_prep_kernel(cq_ref, ckv_ref, kre_ref, pos_ref, qan_ref, kvan_ref, wq_ref, wkv_ref,
                     gq_ref, gk_ref, freq_ref, q_ref, k_ref, v_ref, cqn, ckvn):
    tm = cq_ref.shape[0]
    n_heads = wq_ref.shape[0]

    def latent_norm(ref, g_ref):
        c = ref[...].astype(F32)
        ms = jnp.mean(c * c, axis=-1, keepdims=True)
        return (c * lax.rsqrt(ms + EPS) * g_ref[...]).astype(BF16)

    cqn[...] = latent_norm(cq_ref, qan_ref)
    ckvn[...] = latent_norm(ckv_ref, kvan_ref)

    lane = lax.broadcasted_iota(jnp.int32, (tm, LANES), 1)
    lo = lane < QK_ROPE
    ang = pos_ref[0].astype(F32) * freq_ref[...]
    cs = jnp.where(lo, jnp.cos(ang), jnp.sin(ang))
    scale = QK_HEAD ** -0.5

    def rope_lanes(x1, g1):
        z = x1 * g1 * cs
        return z + pltpu.roll(z, QK_ROPE, 1)

    kre = kre_ref[...].astype(F32)
    ss_kr = jnp.sum(jnp.where(lo, kre * kre, 0.0), axis=-1, keepdims=True)
    k_rope = jnp.where(lo, rope_lanes(kre, gk_ref[:, QK_NOPE:]), 0.0)

    def head(h, carry):
        qh = jnp.dot(cqn[...], wq_ref[h], preferred_element_type=F32)
        q0, q1 = qh[:, :QK_NOPE], qh[:, QK_NOPE:]
        ss = (jnp.sum(q0 * q0, axis=-1, keepdims=True)
              + jnp.sum(jnp.where(lo, q1 * q1, 0.0), axis=-1, keepdims=True))
        inv = lax.rsqrt(ss * (1.0 / QK_HEAD) + EPS) * scale
        q_ref[0, h] = jnp.concatenate(
            [q0 * gq_ref[:, :QK_NOPE] * inv, rope_lanes(q1, gq_ref[:, QK_NOPE:]) * inv],
            axis=1).astype(q_ref.dtype)

        kvh = jnp.dot(ckvn[...], wkv_ref[h], preferred_element_type=F32)
        kn = kvh[:, :QK_NOPE]
        ssk = jnp.sum(kn * kn, axis=-1, keepdims=True) + ss_kr
        invk = lax.rsqrt(ssk * (1.0 / QK_HEAD) + EPS)
        k_ref[0, h] = jnp.concatenate([kn * gk_ref[:, :QK_NOPE] * invk, k_rope * invk],
                                      axis=1).astype(k_ref.dtype)
        v_ref[0, h] = kvh[:, QK_NOPE:].astype(v_ref.dtype)
        return carry

    lax.fori_loop(0, n_heads, head, 0)


def mla_prep(proj, positions, q_a_norm, kv_a_norm, wq, wkv, gq, gk, freq, batch, seq, tm,
             cq_blk, ckv_blk, kre_blk):
    H, q_lora, _ = wq.shape
    kv_lora = wkv.shape[1]
    per_b = seq // tm
    dq = QK_NOPE + LANES
    row = lambda blk: (lambda b, i: (b * per_b + i, blk))
    const2 = lambda b, i: (0, 0)
    const3 = lambda b, i: (0, 0, 0)
    out_map = lambda b, i: (b, 0, i, 0)
    return pl.pallas_call(
        _mla_prep_kernel,
        grid=(batch, per_b),
        in_specs=[pl.BlockSpec((tm, q_lora), row(cq_blk)),
                  pl.BlockSpec((tm, kv_lora), row(ckv_blk)),
                  pl.BlockSpec((tm, LANES), row(kre_blk)),
                  pl.BlockSpec((1, tm, 1), lambda b, i: (b, i, 0)),
                  pl.BlockSpec((1, q_lora), const2),
                  pl.BlockSpec((1, kv_lora), const2),
                  pl.BlockSpec((H, q_lora, dq), const3),
                  pl.BlockSpec((H, kv_lora, QK_NOPE + V_HEAD), const3),
                  pl.BlockSpec((1, dq), const2),
                  pl.BlockSpec((1, dq), const2),
                  pl.BlockSpec((1, LANES), const2)],
        out_specs=[pl.BlockSpec((1, H, tm, dq), out_map),
                   pl.BlockSpec((1, H, tm, dq), out_map),
                   pl.BlockSpec((1, H, tm, V_HEAD), out_map)],
        out_shape=[jax.ShapeDtypeStruct((batch, H, seq, dq), BF16),
                   jax.ShapeDtypeStruct((batch, H, seq, dq), BF16),
                   jax.ShapeDtypeStruct((batch, H, seq, V_HEAD), BF16)],
        scratch_shapes=[pltpu.VMEM((tm, q_lora), BF16), pltpu.VMEM((tm, kv_lora), BF16)],
        compiler_params=_cparams(("arbitrary", "arbitrary")),
        name="mla_prep",
    )(proj, proj, proj, positions.reshape(batch, seq, 1), q_a_norm.reshape(1, q_lora),
      kv_a_norm.reshape(1, kv_lora), wq, wkv, gq, gk, freq)


def _attn_kernel(q_ref, k_ref, v_ref, o_ref, m_ref, l_ref, acc_ref):
    tq = q_ref.shape[2]
    i = pl.program_id(2)
    q = q_ref[0, 0]
    m_ref[...] = jnp.full(m_ref.shape, -jnp.inf, F32)
    l_ref[...] = jnp.zeros(l_ref.shape, F32)
    acc_ref[...] = jnp.zeros(acc_ref.shape, F32)

    def step(j, masked):
        start = pl.multiple_of(j * tq, tq)
        k = k_ref[0, 0, pl.ds(start, tq), :]
        v = v_ref[0, 0, pl.ds(start, tq), :]
        s = lax.dot_general(q, k, (((1,), (1,)), ((), ())), preferred_element_type=F32)
        if masked:
            qc = lax.broadcasted_iota(jnp.int32, s.shape, 0) // CHUNK
            kc = lax.broadcasted_iota(jnp.int32, s.shape, 1) // CHUNK
            s = jnp.where(kc <= qc, s, -jnp.inf)
        m_prev = m_ref[...]
        m_new = jnp.maximum(m_prev, jnp.max(s, axis=-1, keepdims=True))
        alpha = jnp.exp(m_prev - m_new)
        p = jnp.exp(s - m_new)
        l_ref[...] = alpha * l_ref[...] + jnp.sum(p, axis=-1, keepdims=True)
        acc_ref[...] = alpha * acc_ref[...] + jnp.dot(p.astype(v.dtype), v, preferred_element_type=F32)
        m_ref[...] = m_new

    def full_step(j, carry):
        step(j, False)
        return carry

    lax.fori_loop(0, i, full_step, 0)
    step(i, True)
    o_ref[0] = (acc_ref[...] / l_ref[...]).astype(o_ref.dtype)


def mla_attention(q, k, v, tq):
    B, H, S, dq = q.shape
    dv = v.shape[-1]
    assert tq % CHUNK == 0 and S % tq == 0
    return pl.pallas_call(
        _attn_kernel,
        grid=(B, H, S // tq),
        in_specs=[pl.BlockSpec((1, 1, tq, dq), lambda b, h, i: (b, h, i, 0)),
                  pl.BlockSpec((1, 1, S, dq), lambda b, h, i: (b, h, 0, 0)),
                  pl.BlockSpec((1, 1, S, dv), lambda b, h, i: (b, h, 0, 0))],
        out_specs=pl.BlockSpec((1, tq, dv), lambda b, h, i: (b, i, h)),
        out_shape=jax.ShapeDtypeStruct((B, S, H * dv), BF16),
        scratch_shapes=[pltpu.VMEM((tq, 1), F32), pltpu.VMEM((tq, 1), F32),
                        pltpu.VMEM((tq, dv), F32)],
        compiler_params=_cparams(("arbitrary", "arbitrary", "arbitrary")),
        name="mla_attention",
    )(q, k, v)


def _merge_route_kernel(yr_ref, ym_ref, gr_ref, gm_ref, x_ref, mod_ref, wr_ref, wm_ref, wo_ref,
                        n2_ref, wrt_ref, rb_ref,
                        x1_ref, v2_ref, sel_e_ref, sel_r_ref, sel_w_ref, cnt_ref, cnt_acc,
                        *, n_experts):
    tm = x_ref.shape[0]

    @pl.when(pl.program_id(0) == 0)
    def _():
        cnt_acc[...] = jnp.zeros_like(cnt_acc)

    a = jnp.dot(yr_ref[...], wr_ref[...], preferred_element_type=F32)
    b = jnp.dot(ym_ref[...], wm_ref[...], preferred_element_type=F32)
    merged = (jax.nn.sigmoid(gr_ref[...].astype(F32)) * a
              + jax.nn.sigmoid(gm_ref[...].astype(F32)) * b)
    c = jnp.dot(merged.astype(BF16), wo_ref[...], preferred_element_type=F32)
    x1 = x_ref[...] + mod_ref[0, 2:3, :] * c
    x1_ref[...] = x1

    ms = jnp.mean(x1 * x1, axis=-1, keepdims=True)
    v = x1 * lax.rsqrt(ms + EPS) * n2_ref[...]
    v = v * (1.0 + mod_ref[0, 4:5, :]) + mod_ref[0, 3:4, :]
    v2_ref[...] = _pack_rows(v)

    logits = jnp.dot(v, wrt_ref[...], preferred_element_type=F32, precision=lax.Precision.HIGHEST)
    scores = jax.nn.sigmoid(logits)
    lane = lax.broadcasted_iota(jnp.int32, (tm, LANES), 1)
    biased = jnp.where(lane < n_experts, scores + rb_ref[...], -jnp.inf)
    picked = jnp.zeros((tm, LANES), jnp.bool_)
    idxs, tops = [], []
    for _ in range(TOP_K):
        mx = jnp.max(biased, axis=-1, keepdims=True)
        idx = jnp.min(jnp.where(biased == mx, lane, LANES), axis=-1, keepdims=True)
        hit = lane == idx
        idxs.append(idx)
        tops.append(jnp.sum(jnp.where(hit, scores, 0.0), axis=-1, keepdims=True))
        biased = jnp.where(hit, -jnp.inf, biased)
        picked = jnp.logical_or(picked, hit)
    denom = tops[0]
    for t in tops[1:]:
        denom = denom + t

    onehot = jnp.where(picked, 1.0, 0.0)
    r_i = lax.broadcasted_iota(jnp.int32, (tm, tm), 0)
    c_i = lax.broadcasted_iota(jnp.int32, (tm, tm), 1)
    tri = jnp.where(c_i < r_i, 1.0, 0.0).astype(BF16)
    rank = jnp.dot(tri, onehot.astype(BF16), preferred_element_type=F32) + cnt_acc[0:1, :]
    cnt_acc[0:1, :] = cnt_acc[0:1, :] + jnp.sum(onehot, axis=0, keepdims=True)
    cnt_ref[...] = cnt_acc[...]

    sel_e = jnp.zeros((tm, LANES), jnp.int32)
    sel_r = jnp.zeros((tm, LANES), jnp.int32)
    sel_w = jnp.zeros((tm, LANES), F32)
    for kk in range(TOP_K):
        here = lane == kk
        rk = jnp.sum(jnp.where(lane == idxs[kk], rank, 0.0), axis=-1, keepdims=True)
        sel_e = jnp.where(here, idxs[kk], sel_e)
        sel_r = jnp.where(here, rk.astype(jnp.int32), sel_r)
        sel_w = jnp.where(here, tops[kk] / denom * ROUTED_SCALE, sel_w)
    sel_e_ref[...] = sel_e
    sel_r_ref[...] = sel_r
    sel_w_ref[...] = sel_w


def merge_and_route(y_rnn, y_mla, proj, gr_blk, gm_blk, x2, mod, w_rnn_out, w_mla_out, w_out, norm2,
                    w_router_pad, router_bias_pad, n_experts, seq, tm):
    T, D = x2.shape
    per_b = seq // tm
    row = lambda blk: (lambda i: (i, blk))
    const = lambda i: (0, 0)
    resident = lambda shape: pl.BlockSpec(shape, const, pipeline_mode=pl.Buffered(1))
    lanes_out = lambda: pl.BlockSpec((tm, LANES), row(0))
    return pl.pallas_call(
        functools.partial(_merge_route_kernel, n_experts=n_experts),
        grid=(T // tm,),
        in_specs=[pl.BlockSpec((tm, D), row(0)),
                  pl.BlockSpec((tm, D), row(0)),
                  pl.BlockSpec((tm, D), row(gr_blk)),
                  pl.BlockSpec((tm, D), row(gm_blk)),
                  pl.BlockSpec((tm, D), row(0)),
                  pl.BlockSpec((1,) + mod.shape[1:], lambda i: (i // per_b, 0, 0)),
                  resident((D, D)), resident((D, D)), resident((D, D)),
                  pl.BlockSpec((1, D), const),
                  resident((D, LANES)),
                  pl.BlockSpec((1, LANES), const)],
        out_specs=[pl.BlockSpec((tm, D), row(0)),
                   pl.BlockSpec((tm, D // 2), row(0)),
                   lanes_out(), lanes_out(), lanes_out(),
                   pl.BlockSpec((SUBLANES, LANES), const)],
        out_shape=[jax.ShapeDtypeStruct((T, D), F32),
                   jax.ShapeDtypeStruct((T, D // 2), jnp.uint32),
                   jax.ShapeDtypeStruct((T, LANES), jnp.int32),
                   jax.ShapeDtypeStruct((T, LANES), jnp.int32),
                   jax.ShapeDtypeStruct((T, LANES), F32),
                   jax.ShapeDtypeStruct((SUBLANES, LANES), F32)],
        scratch_shapes=[pltpu.VMEM((SUBLANES, LANES), F32)],
        compiler_params=_cparams(("arbitrary",)),
        name="merge_and_route",
    )(y_rnn, y_mla, proj, proj, x2, mod, w_rnn_out, w_mla_out, w_out, norm2.reshape(1, D),
      w_router_pad, router_bias_pad)


def _pack_rows(x):
    half = x.shape[1] // 2
    return pltpu.pack_elementwise([x[:, :half], x[:, half:]], packed_dtype=BF16)


def _unpack_rows(p):
    lo = pltpu.unpack_elementwise(p, index=0, packed_dtype=BF16, unpacked_dtype=F32)
    hi = pltpu.unpack_elementwise(p, index=1, packed_dtype=BF16, unpacked_dtype=F32)
    return lo, hi


def _unpack_rows_bf16(p):
    lo, hi = _unpack_rows(p)
    return jnp.concatenate([lo.astype(BF16), hi.astype(BF16)], axis=1)


def _swiglu(x, wg, wu, wd):
    g = jnp.dot(x, wg, preferred_element_type=F32)
    u = jnp.dot(x, wu, preferred_element_type=F32)
    h = (g * jax.nn.sigmoid(g)) * u
    return jnp.dot(h.astype(BF16), wd, preferred_element_type=F32)


def _row_copy(src, src_row, dst, dst_row, sem):
    return pltpu.make_async_copy(src.at[pl.ds(src_row, 1), :], dst.at[pl.ds(dst_row, 1), :], sem)


def _dispatch_kernel(dest_ref, fill_ref, v_ref, xs_hbm, zrow, sem, zsem, *, n_experts):
    tm = v_ref.shape[0]
    n_copies = tm * TOP_K

    def start(r, c):
        for kk in range(TOP_K):
            _row_copy(v_ref, r, xs_hbm, dest_ref[r * DEST_SLOTS + kk], sem).start()
        return c

    lax.fori_loop(0, tm, start, 0)

    @pl.when(pl.program_id(0) == 0)
    def _():
        zrow[...] = jnp.zeros_like(zrow)

        def per_expert(e, c):
            first, count = fill_ref[e], fill_ref[n_experts + e]

            def one(r, c2):
                _row_copy(zrow, 0, xs_hbm, first + r, zsem).start()
                return c2

            def one_wait(r, c2):
                _row_copy(zrow, 0, xs_hbm, first, zsem).wait()
                return c2

            lax.fori_loop(0, count, one, 0)
            lax.fori_loop(0, count, one_wait, 0)
            return c

        lax.fori_loop(0, n_experts, per_expert, 0)

        def block_copy(b):
            return pltpu.make_async_copy(zrow, xs_hbm.at[pl.ds(b * MOE_BLOCK, MOE_BLOCK), :], zsem)

        def tail(b, c):
            block_copy(b).start()
            block_copy(b).wait()
            return c

        lax.fori_loop(fill_ref[2 * n_experts], xs_hbm.shape[0] // MOE_BLOCK, tail, 0)

    pltpu.make_async_copy(xs_hbm.at[pl.ds(0, n_copies), :], xs_hbm.at[pl.ds(0, n_copies), :], sem).wait()


def moe_dispatch(v2p, dest_flat, fill, n_rows, n_experts, tm):
    T, dh = v2p.shape
    return pl.pallas_call(
        functools.partial(_dispatch_kernel, n_experts=n_experts),
        grid=(T // tm,),
        in_specs=[pl.BlockSpec((tm * DEST_SLOTS,), lambda i: (i,), memory_space=pltpu.SMEM),
                  pl.BlockSpec(memory_space=pltpu.SMEM),
                  pl.BlockSpec((tm, dh), lambda i: (i, 0))],
        out_specs=pl.BlockSpec(memory_space=pl.ANY),
        out_shape=jax.ShapeDtypeStruct((n_rows, dh), jnp.uint32),
        scratch_shapes=[pltpu.VMEM((MOE_BLOCK, dh), jnp.uint32),
                        pltpu.SemaphoreType.DMA(()), pltpu.SemaphoreType.DMA(())],
        compiler_params=_cparams(("arbitrary",)),
        name="moe_dispatch",
    )(dest_flat, fill, v2p)


def _expert_ffn_kernel(be_ref, nused_ref, xs_ref, wg_ref, wu_ref, wd_ref, ys_ref):
    used = pl.program_id(0) < nused_ref[0]

    @pl.when(used)
    def _():
        x = _unpack_rows_bf16(xs_ref[...])
        ys_ref[...] = _pack_rows(_swiglu(x, wg_ref[0], wu_ref[0], wd_ref[0]))

    @pl.when(jnp.logical_not(used))
    def _():
        ys_ref[...] = jnp.zeros_like(ys_ref)


def moe_expert_ffn(xs, blk_e, n_used, w_gate, w_up, w_down):
    n_rows, dh = xs.shape
    nb = n_rows // MOE_BLOCK
    E, D, de = w_gate.shape
    blk = lambda i, be, nu: (jnp.minimum(i, nu[0] - 1), 0)
    wmap = lambda i, be, nu: (be[i], 0, 0)
    return pl.pallas_call(
        _expert_ffn_kernel,
        grid_spec=pltpu.PrefetchScalarGridSpec(
            num_scalar_prefetch=2,
            grid=(nb,),
            in_specs=[pl.BlockSpec((MOE_BLOCK, dh), blk),
                      pl.BlockSpec((1, D, de), wmap),
                      pl.BlockSpec((1, D, de), wmap),
                      pl.BlockSpec((1, de, D), wmap)],
            out_specs=pl.BlockSpec((MOE_BLOCK, dh), lambda i, be, nu: (i, 0))),
        out_shape=jax.ShapeDtypeStruct((n_rows, dh), jnp.uint32),
        compiler_params=_cparams(("arbitrary",)),
        name="moe_expert_ffn",
    )(blk_e, n_used, xs, w_gate, w_up, w_down)


def _shared_combine_kernel(dest_ref, v_ref, x1_ref, w_ref, mod_ref, wg_ref, wu_ref, wd_ref, ys_hbm,
                           o_ref, ybuf, sem):
    tm, dh = v_ref.shape
    n_copies = tm * TOP_K

    def start(r, c):
        for kk in range(TOP_K):
            _row_copy(ys_hbm, dest_ref[r * DEST_SLOTS + kk], ybuf.at[kk], r, sem).start()
        return c

    lax.fori_loop(0, tm, start, 0)
    shared = _swiglu(_unpack_rows_bf16(v_ref[...]), wg_ref[...], wu_ref[...], wd_ref[...])
    pltpu.make_async_copy(ys_hbm.at[pl.ds(0, n_copies), :], ys_hbm.at[pl.ds(0, n_copies), :], sem).wait()

    lo, hi = shared[:, :dh], shared[:, dh:]
    for kk in range(TOP_K):
        w = w_ref[:, kk:kk + 1]
        ylo, yhi = _unpack_rows(ybuf[kk])
        lo = lo + w * ylo
        hi = hi + w * yhi
    g2 = mod_ref[0, 5:6, :]
    o_ref[:, :dh] = x1_ref[:, :dh] + g2[:, :dh] * lo
    o_ref[:, dh:] = x1_ref[:, dh:] + g2[:, dh:] * hi


def moe_shared_combine(v2p, x1, sel_w, dest_flat, ys, mod, ws_gate, ws_up, ws_down, seq, tm):
    T, dh = v2p.shape
    D = 2 * dh
    de = ws_gate.shape[1]
    per_b = seq // tm
    const = lambda i: (0, 0)
    resident = lambda shape: pl.BlockSpec(shape, const, pipeline_mode=pl.Buffered(1))
    return pl.pallas_call(
        _shared_combine_kernel,
        grid=(T // tm,),
        in_specs=[pl.BlockSpec((tm * DEST_SLOTS,), lambda i: (i,), memory_space=pltpu.SMEM),
                  pl.BlockSpec((tm, dh), lambda i: (i, 0)),
                  pl.BlockSpec((tm, D), lambda i: (i, 0)),
                  pl.BlockSpec((tm, LANES), lambda i: (i, 0)),
                  pl.BlockSpec((1,) + mod.shape[1:], lambda i: (i // per_b, 0, 0)),
                  resident((D, de)), resident((D, de)), resident((de, D)),
                  pl.BlockSpec(memory_space=pl.ANY)],
        out_specs=pl.BlockSpec((tm, D), lambda i: (i, 0)),
        out_shape=jax.ShapeDtypeStruct((T, D), F32),
        scratch_shapes=[pltpu.VMEM((TOP_K, tm, dh), jnp.uint32), pltpu.SemaphoreType.DMA(())],
        compiler_params=_cparams(("arbitrary",)),
        name="moe_shared_combine",
    )(dest_flat, v2p, x1, sel_w, mod, ws_gate, ws_up, ws_down, ys)


def _in_proj_layout(w_in, d_rnn, q_lora, kv_lora, d_model):
    o = [0]
    for wdt in (d_rnn, d_rnn, q_lora, kv_lora, QK_ROPE, d_model, d_model):
        o.append(o[-1] + wdt)
    xr, yg, cq, ckv, kr, g_rnn, g_mla = (w_in[:, o[i]:o[i + 1]] for i in range(7))
    pieces = [xr, yg, g_rnn, g_mla, cq, ckv, jnp.concatenate([kr, _rot_half(kr)], axis=1)]
    starts, pos = [], 0
    for p in pieces:
        assert pos % p.shape[1] == 0, "column block of the input projection is not aligned to its width"
        starts.append(pos // p.shape[1])
        pos += p.shape[1]
    n_pad = -(-pos // 512) * 512
    pieces.append(jnp.zeros((w_in.shape[0], n_pad - pos), w_in.dtype))
    return jnp.concatenate(pieces, axis=1).astype(BF16), starts


def _routing_tables(counts, sel_e, sel_r, n_tokens, n_experts):
    counts = counts.astype(jnp.int32)
    padded = (counts + MOE_BLOCK - 1) // MOE_BLOCK * MOE_BLOCK
    pad_end = jnp.cumsum(padded)
    pad_start = pad_end - padded
    nb = -(-(n_tokens * TOP_K) // MOE_BLOCK) + n_experts
    n_used = pad_end[-1] // MOE_BLOCK
    blk = jnp.arange(nb, dtype=jnp.int32)
    blk_e = jnp.sum((pad_end[None, :] <= (blk * MOE_BLOCK)[:, None]).astype(jnp.int32), axis=1)
    blk_e = jnp.minimum(blk_e, n_experts - 1)
    last_e = jnp.sum(jnp.where(blk == n_used - 1, blk_e, 0))
    blk_e = jnp.where(blk < n_used, blk_e, last_e)
    e_ids = jnp.arange(n_experts, dtype=jnp.int32)
    se, sr = sel_e[:, :DEST_SLOTS], sel_r[:, :DEST_SLOTS]
    dest = sr + jnp.sum(jnp.where(se[:, :, None] == e_ids, pad_start, 0), axis=-1)
    fill = jnp.concatenate([pad_start + counts, padded - counts, n_used.reshape(1)])
    return dest.reshape(-1), fill, blk_e, n_used.reshape(1), nb * MOE_BLOCK


def _layer(x2, c, positions, batch, seq, w_mod, b_mod, norm1, w_in, conv_w, conv_b, w_a, b_a, w_i, b_i,
           lru_lambda, q_a_norm, kv_a_norm, w_uq, w_ukv, q_norm, k_norm, w_rnn_out, w_mla_out, w_out,
           norm2, w_router, router_bias, w_gate, w_up, w_down, ws_gate, ws_up, ws_down):
    T, D = x2.shape
    d_rnn = conv_w.shape[1]
    q_lora, kv_lora = q_a_norm.shape[0], kv_a_norm.shape[0]
    n_experts = w_router.shape[1]
    assert w_uq.shape[2] == QK_HEAD and w_ukv.shape[2] == QK_NOPE + V_HEAD
    assert d_rnn == D and w_mla_out.shape[0] == D and n_experts <= LANES

    mod = adaln_mod(c, w_mod, b_mod)

    w_cat, (_, _, gr_blk, gm_blk, cq_blk, ckv_blk, kre_blk) = _in_proj_layout(w_in, d_rnn, q_lora, kv_lora, D)
    proj = in_projection(x2, mod, norm1, w_cat, seq, _tile(seq, 1024, SUBLANES), _tile(w_cat.shape[1], 1536))

    wai = jnp.concatenate([w_a, w_i], axis=-1).astype(BF16)
    y_rnn = rglru_branch(proj, conv_w, conv_b, wai, b_a, b_i, lru_lambda, batch, seq, _tile(seq, 256, 16))

    wq, wkv, gq, gk, freq = mla_weights(w_uq, w_ukv, q_norm, k_norm)
    q, k, v = mla_prep(proj, positions, q_a_norm, kv_a_norm, wq, wkv, gq, gk, freq, batch, seq,
                       _tile(seq, 512, 16), cq_blk, ckv_blk, kre_blk)
    y_mla = mla_attention(q, k, v, _tile(seq, 512, CHUNK)).reshape(T, -1)

    w_router_pad = jnp.zeros((D, LANES), F32).at[:, :n_experts].set(w_router)
    router_bias_pad = jnp.zeros((1, LANES), F32).at[0, :n_experts].set(router_bias)
    x1, v2p, sel_e, sel_r, sel_w, cnt = merge_and_route(
        y_rnn, y_mla, proj, gr_blk, gm_blk, x2, mod, w_rnn_out.astype(BF16), w_mla_out.astype(BF16),
        w_out.astype(BF16), norm2, w_router_pad, router_bias_pad, n_experts, seq, _tile(seq, 256, 16))

    dest, fill, blk_e, n_used, n_rows = _routing_tables(cnt[0, :n_experts], sel_e, sel_r, T, n_experts)
    xs = moe_dispatch(v2p, dest, fill, n_rows, n_experts, _tile(seq, 512, 128))
    ys = moe_expert_ffn(xs, blk_e, n_used, w_gate.astype(BF16), w_up.astype(BF16), w_down.astype(BF16))
    return moe_shared_combine(v2p, x1, sel_w, dest, ys, mod, ws_gate.astype(BF16), ws_up.astype(BF16),
                              ws_down.astype(BF16), seq, _tile(seq, 256, 128))


def kernel(x, c, positions, w_mod, b_mod, norm1, w_in, conv_w, conv_b, w_a, b_a, w_i, b_i, lru_lambda,
           q_a_norm, kv_a_norm, w_uq, w_ukv, q_norm, k_norm, w_rnn_out, w_mla_out, w_out, norm2, w_router,
           router_bias, w_gate, w_up, w_down, ws_gate, ws_up, ws_down):
    B, S, D = x.shape
    x2 = x.reshape(B * S, D)
    per_layer = (w_mod, b_mod, norm1, w_in, conv_w, conv_b, w_a, b_a, w_i, b_i, lru_lambda, q_a_norm,
                 kv_a_norm, w_uq, w_ukv, q_norm, k_norm, w_rnn_out, w_mla_out, w_out, norm2, w_router,
                 router_bias, w_gate, w_up, w_down, ws_gate, ws_up, ws_down)
    for l in range(w_mod.shape[0]):
        x2 = _layer(x2, c, positions, B, S, *(w[l] for w in per_layer))
    return x2.reshape(B, S, D)
```

```python
import functools

import jax
import jax.numpy as jnp
from jax import lax
from jax.experimental import pallas as pl
from jax.experimental.pallas import tpu as pltpu

F32 = jnp.float32
BF16 = jnp.bfloat16

EPS = 1e-6
CHUNK = 64
LRU_C = 8.0
ROPE_THETA = 10000.0
TOP_K = 6
ROUTED_SCALE = 2.5
RNN_BLOCK_DIM = 128

LANES = 128
SUBLANES = 8
VMEM_LIMIT_BYTES = 56 * 1024 * 1024

MOE_BLOCK = 256
DEST_SLOTS = 8


def _cparams(semantics):
    return pltpu.CompilerParams(dimension_semantics=semantics, vmem_limit_bytes=VMEM_LIMIT_BYTES)


def _tile(n, target, align=LANES):
    if n <= target:
        return n
    t = (target // align) * align
    while t > align and n % t:
        t -= align
    assert n % t == 0, (n, target, align)
    return t


def _mod_kernel(c_ref, w_ref, b_ref, o_ref):
    c = c_ref[...]
    s = c * jax.nn.sigmoid(c)
    o_ref[...] = jnp.dot(s, w_ref[...], preferred_element_type=F32,
                         precision=lax.Precision.HIGHEST) + b_ref[...]


def adaln_mod(c, w_mod, b_mod):
    B, D = c.shape
    N = w_mod.shape[1]
    rows = -(-B // SUBLANES) * SUBLANES
    c_pad = jnp.zeros((rows, D), F32).at[:B].set(c)
    tn = _tile(N, 1024)
    out = pl.pallas_call(
        _mod_kernel,
        grid=(N // tn,),
        in_specs=[pl.BlockSpec((rows, D), lambda j: (0, 0)),
                  pl.BlockSpec((D, tn), lambda j: (0, j)),
                  pl.BlockSpec((1, tn), lambda j: (0, j))],
        out_specs=pl.BlockSpec((rows, tn), lambda j: (0, j)),
        out_shape=jax.ShapeDtypeStruct((rows, N), F32),
        compiler_params=_cparams(("arbitrary",)),
        name="adaln_mod",
    )(c_pad, w_mod, b_mod.reshape(1, N))
    return out[:B].reshape(B, N // D, D)


def _inproj_kernel(x_ref, mod_ref, g_ref, w_ref, o_ref, u_ref):
    @pl.when(pl.program_id(1) == 0)
    def _():
        x = x_ref[...]
        ms = jnp.mean(x * x, axis=-1, keepdims=True)
        y = x * lax.rsqrt(ms + EPS) * g_ref[...]
        u_ref[...] = (y * (1.0 + mod_ref[0, 1:2, :]) + mod_ref[0, 0:1, :]).astype(BF16)

    o_ref[...] = jnp.dot(u_ref[...], w_ref[...], preferred_element_type=F32).astype(o_ref.dtype)


def in_projection(x2, mod, norm1, w_cat, seq, tm, tn):
    T, D = x2.shape
    N = w_cat.shape[1]
    per_b = seq // tm
    return pl.pallas_call(
        _inproj_kernel,
        grid=(T // tm, N // tn),
        in_specs=[pl.BlockSpec((tm, D), lambda i, j: (i, 0)),
                  pl.BlockSpec((1,) + mod.shape[1:], lambda i, j: (i // per_b, 0, 0)),
                  pl.BlockSpec((1, D), lambda i, j: (0, 0)),
                  pl.BlockSpec((D, tn), lambda i, j: (0, j))],
        out_specs=pl.BlockSpec((tm, tn), lambda i, j: (i, j)),
        out_shape=jax.ShapeDtypeStruct((T, N), BF16),
        scratch_shapes=[pltpu.VMEM((tm, D), BF16)],
        compiler_params=_cparams(("arbitrary", "arbitrary")),
        name="in_projection",
    )(x2, mod, norm1.reshape(1, D), w_cat)


def _gelu_tanh(x):
    return 0.5 * x * (1.0 + jnp.tanh(0.7978845608028654 * (x + 0.044715 * (x * x * x))))


def _rglru_kernel(xr_ref, yg_ref, cw_ref, cb_ref, wai_ref, ba_ref, bi_ref, lam_ref, o_ref,
                  xbuf, xcbuf, abuf, bbuf, hc, *, conv_width):
    ts, dr = xcbuf.shape
    halo = SUBLANES
    nblk = dr // RNN_BLOCK_DIM

    @pl.when(pl.program_id(1) == 0)
    def _():
        xbuf[0:halo, :] = jnp.zeros((halo, dr), F32)
        hc[...] = jnp.zeros_like(hc)

    xbuf[halo:halo + ts, :] = xr_ref[...].astype(F32)
    xc = cb_ref[...] + xbuf[halo:halo + ts, :] * cw_ref[conv_width - 1:conv_width, :]
    for j in range(conv_width - 1):
        off = halo - (conv_width - 1) + j
        xc = xc + xbuf[off:off + ts, :] * cw_ref[j:j + 1, :]
    xcbuf[...] = xc
    xbuf[0:halo, :] = xbuf[ts:ts + halo, :]

    z = -lam_ref[...]
    sp = jnp.maximum(z, 0.0) + jnp.log1p(jnp.exp(-jnp.abs(z)))
    for k in range(nblk):
        cols = slice(k * RNN_BLOCK_DIM, (k + 1) * RNN_BLOCK_DIM)
        xk = xcbuf[:, cols]
        g = jnp.dot(xk.astype(BF16), wai_ref[k], preferred_element_type=F32)
        r = jax.nn.sigmoid(g[:, :RNN_BLOCK_DIM] + ba_ref[:, cols])
        ig = jax.nn.sigmoid(g[:, RNN_BLOCK_DIM:] + bi_ref[:, cols])
        a = jnp.exp((-LRU_C) * r * sp[:, cols])
        abuf[:, cols] = a
        bbuf[:, cols] = jnp.sqrt(1.0 - a * a) * (ig * xk)

    row = lax.broadcasted_iota(jnp.int32, (SUBLANES, dr), 0)
    rows2 = 2 * SUBLANES

    def scan8(a, b, h):
        for s in (1, 2, 4):
            m = row >= s
            a_s = jnp.where(m, pltpu.roll(a, s, 0), 1.0)
            b_s = jnp.where(m, pltpu.roll(b, s, 0), 0.0)
            b = a * b_s + b
            a = a * a_s
        hh = a * h + b
        return hh, jnp.broadcast_to(hh[SUBLANES - 1:SUBLANES, :], (SUBLANES, dr))

    def body(c, h):
        r0 = pl.multiple_of(c * rows2, rows2)
        a16 = abuf[pl.ds(r0, rows2), :]
        b16 = bbuf[pl.ds(r0, rows2), :]
        h0, h = scan8(a16[:SUBLANES], b16[:SUBLANES], h)
        h1, h = scan8(a16[SUBLANES:], b16[SUBLANES:], h)
        hs = jnp.concatenate([h0, h1], axis=0)
        yg = yg_ref[pl.ds(r0, rows2), :].astype(F32)
        o_ref[pl.ds(r0, rows2), :] = (hs * _gelu_tanh(yg)).astype(o_ref.dtype)
        return h

    hc[...] = lax.fori_loop(0, ts // rows2, body, hc[...])


def rglru_branch(proj, conv_w, conv_b, wai, b_a, b_i, lam, batch, seq, ts):
    T = proj.shape[0]
    W, dr = conv_w.shape
    nblk = dr // RNN_BLOCK_DIM
    per_b = seq // ts
    vec = lambda: pl.BlockSpec((1, dr), lambda b, i: (0, 0))
    return pl.pallas_call(
        functools.partial(_rglru_kernel, conv_width=W),
        grid=(batch, per_b),
        in_specs=[pl.BlockSpec((ts, dr), lambda b, i: (b * per_b + i, 0)),
                  pl.BlockSpec((ts, dr), lambda b, i: (b * per_b + i, 1)),
                  pl.BlockSpec((W, dr), lambda b, i: (0, 0)),
                  vec(),
                  pl.BlockSpec((nblk, RNN_BLOCK_DIM, 2 * RNN_BLOCK_DIM), lambda b, i: (0, 0, 0)),
                  vec(), vec(), vec()],
        out_specs=pl.BlockSpec((ts, dr), lambda b, i: (b * per_b + i, 0)),
        out_shape=jax.ShapeDtypeStruct((T, dr), BF16),
        scratch_shapes=[pltpu.VMEM((ts + SUBLANES, dr), F32),
                        pltpu.VMEM((ts, dr), F32),
                        pltpu.VMEM((ts, dr), F32),
                        pltpu.VMEM((ts, dr), F32),
                        pltpu.VMEM((SUBLANES, dr), F32)],
        compiler_params=_cparams(("arbitrary", "arbitrary")),
        name="rglru_branch",
    )(proj, proj, conv_w, conv_b.reshape(1, dr), wai, b_a.reshape(1, dr), b_i.reshape(1, dr),
      lam.reshape(1, dr))


QK_NOPE = 128
QK_ROPE = 64
V_HEAD = 128
QK_HEAD = QK_NOPE + QK_ROPE


def _rot_half(w):
    half = QK_ROPE // 2
    return jnp.concatenate([-w[..., half:], w[..., :half]], axis=-1)


def mla_weights(w_uq, w_ukv, q_norm, k_norm):
    r, H, _ = w_uq.shape
    half = QK_ROPE // 2
    pair = lambda w: w.reshape(w.shape[0], H // 2, 2 * w.shape[2])
    rope_w = w_uq[..., QK_NOPE:]
    wq = jnp.concatenate([pair(w_uq[..., :QK_NOPE]), pair(rope_w), pair(_rot_half(rope_w))], axis=-1)
    wkv = jnp.concatenate([pair(w_ukv[..., :QK_NOPE]), pair(w_ukv[..., QK_NOPE:])], axis=-1)
    wq = wq.transpose(1, 0, 2).astype(BF16)
    wkv = wkv.transpose(1, 0, 2).astype(BF16)

    def gains(g):
        gr = g[QK_NOPE:]
        swapped = jnp.concatenate([gr[half:], gr[:half]])
        return jnp.stack([g[:QK_NOPE], jnp.tile(gr, 2), jnp.tile(swapped, 2)]).astype(F32)

    inv_freq = ROPE_THETA ** (-jnp.arange(0, QK_ROPE, 2, dtype=F32) / QK_ROPE)
    freq = jnp.tile(inv_freq, 2 * LANES // QK_ROPE).reshape(1, LANES)

    rows = jnp.arange(3 * LANES)[:, None]
    cols = jnp.arange(2 * LANES)[None, :]
    head_b = cols >= LANES
    in_nope = (rows // LANES) == jnp.where(head_b, 1, 0)
    in_rope = (rows >= 2 * LANES) & (((rows - 2 * LANES) // QK_ROPE) == jnp.where(head_b, 1, 0))
    seg_q = (in_nope | in_rope).astype(BF16)
    seg_k = in_nope[:2 * LANES].astype(BF16)
    return wq, wkv, gains(q_norm), gains(k_norm), freq, seg_q, seg_k


def _mla_prep_kernel(cq_ref, ckv_ref, kre_ref, pos_ref, qan_ref, kvan_ref, wq_ref, wkv_ref,
                     gq_ref, gk_ref, freq_ref, segq_ref, segk_ref, q_ref, k_ref, vt_ref, cqn, ckvn):
    tm = cq_ref.shape[0]
    n_pairs = wq_ref.shape[0]

    def latent_norm(ref, g_ref):
        c = ref[...].astype(F32)
        ms = jnp.mean(c * c, axis=-1, keepdims=True)
        return (c * lax.rsqrt(ms + EPS) * g_ref[...]).astype(BF16)

    cqn[...] = latent_norm(cq_ref, qan_ref)
    ckvn[...] = latent_norm(ckv_ref, kvan_ref)

    lane = lax.broadcasted_iota(jnp.int32, (tm, LANES), 1)
    lo = lane < QK_ROPE
    ang = pos_ref[0].astype(F32) * freq_ref[...]
    cos_t, sin_t = jnp.cos(ang), jnp.sin(ang)
    q_cos = gq_ref[1:2, :] * cos_t
    q_sin = gq_ref[2:3, :] * sin_t
    q_scale = (QK_HEAD ** -0.5) * 1.4426950408889634

    kre = kre_ref[...].astype(F32)
    ss_kr = jnp.dot(jnp.where(lo, kre * kre, 0.0).astype(BF16), segk_ref[0:LANES, 0:LANES],
                    preferred_element_type=F32)
    z = kre * jnp.where(lo, gk_ref[1:2, :] * cos_t, gk_ref[2:3, :] * sin_t)
    k_rope = z + pltpu.roll(z, QK_ROPE, 1)
    k_rope_a = jnp.where(lo, k_rope, 0.0)
    k_rope_b = jnp.where(lo, 0.0, k_rope)

    def seg_sums(pieces, seg_ref):
        sq = jnp.concatenate([(p * p).astype(BF16) for p in pieces], axis=1)
        return jnp.dot(sq, seg_ref[...], preferred_element_type=F32)

    def pair(p, carry):
        qp = jnp.dot(cqn[...], wq_ref[p], preferred_element_type=F32)
        qa, qb, rope_raw, rot_raw = (qp[:, j * LANES:(j + 1) * LANES] for j in range(4))
        ss = seg_sums([qa, qb, rope_raw], segq_ref)
        inv = lax.rsqrt(ss * (1.0 / QK_HEAD) + EPS) * q_scale
        rope = rope_raw * q_cos + rot_raw * q_sin
        for j, qn in enumerate((qa, qb)):
            inv_j = inv[:, j * LANES:(j + 1) * LANES]
            q_ref[0, 2 * p + j] = jnp.concatenate(
                [qn * gq_ref[0:1, :] * inv_j, rope * inv_j], axis=1).astype(q_ref.dtype)

        kvp = jnp.dot(ckvn[...], wkv_ref[p], preferred_element_type=F32)
        ka, kb, va, vb = (kvp[:, j * LANES:(j + 1) * LANES] for j in range(4))
        ssk = seg_sums([ka, kb], segk_ref)
        for j, (kn, kr_j, vj) in enumerate(((ka, k_rope_a, va), (kb, k_rope_b, vb))):
            inv_j = lax.rsqrt((ssk[:, j * LANES:(j + 1) * LANES] + ss_kr) * (1.0 / QK_HEAD) + EPS)
            k_ref[0, 2 * p + j] = jnp.concatenate(
                [kn * gk_ref[0:1, :] * inv_j, kr_j * inv_j], axis=1).astype(k_ref.dtype)
            vt_ref[0, 2 * p + j] = vj.T.astype(vt_ref.dtype)
        return carry

    lax.fori_loop(0, n_pairs, pair, 0)


def mla_prep(proj, positions, q_a_norm, kv_a_norm, wq, wkv, gq, gk, freq, seg_q, seg_k, batch, seq, tm,
             cq_blk, ckv_blk, kre_blk):
    n_pairs, q_lora, _ = wq.shape
    kv_lora = wkv.shape[1]
    H = 2 * n_pairs
    per_b = seq // tm
    dq = QK_NOPE + LANES
    row = lambda blk: (lambda b, i: (b * per_b + i, blk))
    const2 = lambda b, i: (0, 0)
    const3 = lambda b, i: (0, 0, 0)
    out_map = lambda b, i: (b, 0, i, 0)
    return pl.pallas_call(
        _mla_prep_kernel,
        grid=(batch, per_b),
        in_specs=[pl.BlockSpec((tm, q_lora), row(cq_blk)),
                  pl.BlockSpec((tm, kv_lora), row(ckv_blk)),
                  pl.BlockSpec((tm, LANES), row(kre_blk)),
                  pl.BlockSpec((1, tm, 1), lambda b, i: (b, i, 0)),
                  pl.BlockSpec((1, q_lora), const2),
                  pl.BlockSpec((1, kv_lora), const2),
                  pl.BlockSpec(wq.shape, const3),
                  pl.BlockSpec(wkv.shape, const3),
                  pl.BlockSpec(gq.shape, const2),
                  pl.BlockSpec(gk.shape, const2),
                  pl.BlockSpec((1, LANES), const2),
                  pl.BlockSpec(seg_q.shape, const2),
                  pl.BlockSpec(seg_k.shape, const2)],
        out_specs=[pl.BlockSpec((1, H, tm, dq), out_map),
                   pl.BlockSpec((1, H, tm, dq), out_map),
                   pl.BlockSpec((1, H, V_HEAD, tm), lambda b, i: (b, 0, 0, i))],
        out_shape=[jax.ShapeDtypeStruct((batch, H, seq, dq), BF16),
                   jax.ShapeDtypeStruct((batch, H, seq, dq), BF16),
                   jax.ShapeDtypeStruct((batch, H, V_HEAD, seq), BF16)],
        scratch_shapes=[pltpu.VMEM((tm, q_lora), BF16), pltpu.VMEM((tm, kv_lora), BF16)],
        compiler_params=_cparams(("arbitrary", "arbitrary")),
        name="mla_prep",
    )(proj, proj, proj, positions.reshape(batch, seq, 1), q_a_norm.reshape(1, q_lora),
      kv_a_norm.reshape(1, kv_lora), wq, wkv, gq, gk, freq, seg_q, seg_k)


def _attn_kernel(q_ref, k_ref, vt_ref, o_ref, qt_ref, m_ref, l_ref, acc_ref):
    nh, tq = q_ref.shape[1], q_ref.shape[2]
    dv = vt_ref.shape[2]
    i = pl.program_id(2)
    m_ref[...] = jnp.full(m_ref.shape, -jnp.inf, F32)
    l_ref[...] = jnp.zeros(l_ref.shape, F32)
    acc_ref[...] = jnp.zeros(acc_ref.shape, F32)
    for h in range(nh):
        qt_ref[h] = q_ref[0, h].astype(F32).T.astype(qt_ref.dtype)

    def block(j, masked):
        start = pl.multiple_of(j * tq, tq)
        scores = [jnp.dot(k_ref[0, h, pl.ds(start, tq), :], qt_ref[h], preferred_element_type=F32)
                  for h in range(nh)]
        for h in range(nh):
            s = scores[h]
            vt = vt_ref[0, h, :, pl.ds(start, tq)]
            if masked:
                kc = lax.broadcasted_iota(jnp.int32, s.shape, 0) // CHUNK
                qc = lax.broadcasted_iota(jnp.int32, s.shape, 1) // CHUNK
                s = jnp.where(kc <= qc, s, -jnp.inf)
            m_prev = m_ref[h]
            m_new = jnp.maximum(m_prev, jnp.max(s, axis=0, keepdims=True))
            alpha = jnp.exp2(m_prev - m_new)
            p = jnp.exp2(s - m_new)
            l_ref[h] = alpha * l_ref[h] + jnp.sum(p, axis=0, keepdims=True)
            acc_ref[h] = alpha * acc_ref[h] + jnp.dot(vt, p.astype(vt.dtype), preferred_element_type=F32)
            m_ref[h] = m_new

    def full_block(j, carry):
        block(j, False)
        return carry

    lax.fori_loop(0, i, full_block, 0)
    block(i, True)
    for h in range(nh):
        o_ref[0, :, h * dv:(h + 1) * dv] = (acc_ref[h] / l_ref[h]).T.astype(o_ref.dtype)


def mla_attention(q, k, vt, tq, nh):
    B, H, S, dq = q.shape
    dv = vt.shape[2]
    assert tq % CHUNK == 0 and S % tq == 0 and H % nh == 0
    head_map = lambda b, g, i: (b, g, 0, 0)
    return pl.pallas_call(
        _attn_kernel,
        grid=(B, H // nh, S // tq),
        in_specs=[pl.BlockSpec((1, nh, tq, dq), lambda b, g, i: (b, g, i, 0)),
                  pl.BlockSpec((1, nh, S, dq), head_map, pipeline_mode=pl.Buffered(1)),
                  pl.BlockSpec((1, nh, dv, S), head_map, pipeline_mode=pl.Buffered(1))],
        out_specs=pl.BlockSpec((1, tq, nh * dv), lambda b, g, i: (b, i, g)),
        out_shape=jax.ShapeDtypeStruct((B, S, H * dv), BF16),
        scratch_shapes=[pltpu.VMEM((nh, dq, tq), BF16),
                        pltpu.VMEM((nh, 1, tq), F32), pltpu.VMEM((nh, 1, tq), F32),
                        pltpu.VMEM((nh, dv, tq), F32)],
        compiler_params=_cparams(("arbitrary", "arbitrary", "arbitrary")),
        name="mla_attention",
    )(q, k, vt)


def _merge_route_kernel(yr_ref, ym_ref, gr_ref, gm_ref, x_ref, mod_ref, wr_ref, wm_ref, wo_ref,
                        n2_ref, wrt_ref, rb_ref,
                        x1_ref, v2_ref, sel_e_ref, sel_r_ref, sel_w_ref, cnt_ref, cnt_acc,
                        *, n_experts):
    tm = x_ref.shape[0]

    @pl.when(pl.program_id(0) == 0)
    def _():
        cnt_acc[...] = jnp.zeros_like(cnt_acc)

    a = jnp.dot(yr_ref[...], wr_ref[...], preferred_element_type=F32)
    b = jnp.dot(ym_ref[...], wm_ref[...], preferred_element_type=F32)
    merged = (jax.nn.sigmoid(gr_ref[...].astype(F32)) * a
              + jax.nn.sigmoid(gm_ref[...].astype(F32)) * b)
    c = jnp.dot(merged.astype(BF16), wo_ref[...], preferred_element_type=F32)
    x1 = x_ref[...] + mod_ref[0, 2:3, :] * c
    x1_ref[...] = x1

    ms = jnp.mean(x1 * x1, axis=-1, keepdims=True)
    v = x1 * lax.rsqrt(ms + EPS) * n2_ref[...]
    v = v * (1.0 + mod_ref[0, 4:5, :]) + mod_ref[0, 3:4, :]
    v2_ref[...] = _pack_rows(v)

    v_hi = v.astype(BF16)
    v_lo = (v - v_hi.astype(F32)).astype(BF16)
    logits = (jnp.dot(v_hi, wrt_ref[0], preferred_element_type=F32)
              + jnp.dot(v_hi, wrt_ref[1], preferred_element_type=F32)
              + jnp.dot(v_lo, wrt_ref[0], preferred_element_type=F32))
    scores = jax.nn.sigmoid(logits)
    lane = lax.broadcasted_iota(jnp.int32, (tm, LANES), 1)
    biased = jnp.where(lane < n_experts, scores + rb_ref[...], -jnp.inf)
    picked = jnp.zeros((tm, LANES), jnp.bool_)
    idxs, tops = [], []
    for _ in range(TOP_K):
        mx = jnp.max(biased, axis=-1, keepdims=True)
        idx = jnp.min(jnp.where(biased == mx, lane, LANES), axis=-1, keepdims=True)
        hit = lane == idx
        idxs.append(idx)
        tops.append(jnp.sum(jnp.where(hit, scores, 0.0), axis=-1, keepdims=True))
        biased = jnp.where(hit, -jnp.inf, biased)
        picked = jnp.logical_or(picked, hit)
    denom = tops[0]
    for t in tops[1:]:
        denom = denom + t

    onehot = jnp.where(picked, 1.0, 0.0)
    r_i = lax.broadcasted_iota(jnp.int32, (tm, tm), 0)
    c_i = lax.broadcasted_iota(jnp.int32, (tm, tm), 1)
    tri = jnp.where(c_i < r_i, 1.0, 0.0).astype(BF16)
    rank = jnp.dot(tri, onehot.astype(BF16), preferred_element_type=F32) + cnt_acc[0:1, :]
    cnt_acc[0:1, :] = cnt_acc[0:1, :] + jnp.sum(onehot, axis=0, keepdims=True)
    cnt_ref[...] = cnt_acc[...]

    sel_e = jnp.zeros((tm, LANES), jnp.int32)
    sel_r = jnp.zeros((tm, LANES), jnp.int32)
    sel_w = jnp.zeros((tm, LANES), F32)
    for kk in range(TOP_K):
        here = lane == kk
        rk = jnp.sum(jnp.where(lane == idxs[kk], rank, 0.0), axis=-1, keepdims=True)
        sel_e = jnp.where(here, idxs[kk], sel_e)
        sel_r = jnp.where(here, rk.astype(jnp.int32), sel_r)
        sel_w = jnp.where(here, tops[kk] / denom * ROUTED_SCALE, sel_w)
    sel_e_ref[...] = sel_e
    sel_r_ref[...] = sel_r
    sel_w_ref[...] = sel_w


def merge_and_route(y_rnn, y_mla, proj, gr_blk, gm_blk, x2, mod, w_rnn_out, w_mla_out, w_out, norm2,
                    w_router_pad, router_bias_pad, n_experts, seq, tm):
    T, D = x2.shape
    per_b = seq // tm
    row = lambda blk: (lambda i: (i, blk))
    const = lambda i: (0, 0)
    resident = lambda shape: pl.BlockSpec(shape, const, pipeline_mode=pl.Buffered(1))
    lanes_out = lambda: pl.BlockSpec((tm, LANES), row(0))
    return pl.pallas_call(
        functools.partial(_merge_route_kernel, n_experts=n_experts),
        grid=(T // tm,),
        in_specs=[pl.BlockSpec((tm, D), row(0)),
                  pl.BlockSpec((tm, D), row(0)),
                  pl.BlockSpec((tm, D), row(gr_blk)),
                  pl.BlockSpec((tm, D), row(gm_blk)),
                  pl.BlockSpec((tm, D), row(0)),
                  pl.BlockSpec((1,) + mod.shape[1:], lambda i: (i // per_b, 0, 0)),
                  resident((D, D)), resident((D, D)), resident((D, D)),
                  pl.BlockSpec((1, D), const),
                  pl.BlockSpec((2, D, LANES), lambda i: (0, 0, 0)),
                  pl.BlockSpec((1, LANES), const)],
        out_specs=[pl.BlockSpec((tm, D), row(0)),
                   pl.BlockSpec((tm, D // 2), row(0)),
                   lanes_out(), lanes_out(), lanes_out(),
                   pl.BlockSpec((SUBLANES, LANES), const)],
        out_shape=[jax.ShapeDtypeStruct((T, D), F32),
                   jax.ShapeDtypeStruct((T, D // 2), jnp.uint32),
                   jax.ShapeDtypeStruct((T, LANES), jnp.int32),
                   jax.ShapeDtypeStruct((T, LANES), jnp.int32),
                   jax.ShapeDtypeStruct((T, LANES), F32),
                   jax.ShapeDtypeStruct((SUBLANES, LANES), F32)],
        scratch_shapes=[pltpu.VMEM((SUBLANES, LANES), F32)],
        compiler_params=_cparams(("arbitrary",)),
        name="merge_and_route",
    )(y_rnn, y_mla, proj, proj, x2, mod, w_rnn_out, w_mla_out, w_out, norm2.reshape(1, D),
      w_router_pad, router_bias_pad)


def _pack_rows(x):
    half = x.shape[1] // 2
    return pltpu.pack_elementwise([x[:, :half], x[:, half:]], packed_dtype=BF16)


def _unpack_rows(p):
    lo = pltpu.unpack_elementwise(p, index=0, packed_dtype=BF16, unpacked_dtype=F32)
    hi = pltpu.unpack_elementwise(p, index=1, packed_dtype=BF16, unpacked_dtype=F32)
    return lo, hi


def _unpack_rows_bf16(p):
    lo, hi = _unpack_rows(p)
    return jnp.concatenate([lo.astype(BF16), hi.astype(BF16)], axis=1)


def _swiglu(x, wg, wu, wd):
    g = jnp.dot(x, wg, preferred_element_type=F32)
    u = jnp.dot(x, wu, preferred_element_type=F32)
    h = (g * jax.nn.sigmoid(g)) * u
    return jnp.dot(h.astype(BF16), wd, preferred_element_type=F32)


def _row_copy(src, src_row, dst, dst_row, sem):
    return pltpu.make_async_copy(src.at[pl.ds(src_row, 1), :], dst.at[pl.ds(dst_row, 1), :], sem)


def _dispatch_kernel(dest_ref, fill_ref, v_ref, xs_hbm, zrow, sem, zsem, *, n_experts):
    tm = v_ref.shape[0]
    n_copies = tm * TOP_K

    def start(r, c):
        for kk in range(TOP_K):
            _row_copy(v_ref, r, xs_hbm, dest_ref[r * DEST_SLOTS + kk], sem).start()
        return c

    lax.fori_loop(0, tm, start, 0)

    @pl.when(pl.program_id(0) == 0)
    def _():
        zrow[...] = jnp.zeros_like(zrow)

        def per_expert(e, c):
            first, count = fill_ref[e], fill_ref[n_experts + e]

            def one(r, c2):
                _row_copy(zrow, 0, xs_hbm, first + r, zsem).start()
                return c2

            def one_wait(r, c2):
                _row_copy(zrow, 0, xs_hbm, first, zsem).wait()
                return c2

            lax.fori_loop(0, count, one, 0)
            lax.fori_loop(0, count, one_wait, 0)
            return c

        lax.fori_loop(0, n_experts, per_expert, 0)

        def block_copy(b):
            return pltpu.make_async_copy(zrow, xs_hbm.at[pl.ds(b * MOE_BLOCK, MOE_BLOCK), :], zsem)

        def tail(b, c):
            block_copy(b).start()
            block_copy(b).wait()
            return c

        lax.fori_loop(fill_ref[2 * n_experts], xs_hbm.shape[0] // MOE_BLOCK, tail, 0)

    pltpu.make_async_copy(xs_hbm.at[pl.ds(0, n_copies), :], xs_hbm.at[pl.ds(0, n_copies), :], sem).wait()


def moe_dispatch(v2p, dest_flat, fill, n_rows, n_experts, tm):
    T, dh = v2p.shape
    return pl.pallas_call(
        functools.partial(_dispatch_kernel, n_experts=n_experts),
        grid=(T // tm,),
        in_specs=[pl.BlockSpec((tm * DEST_SLOTS,), lambda i: (i,), memory_space=pltpu.SMEM),
                  pl.BlockSpec(memory_space=pltpu.SMEM),
                  pl.BlockSpec((tm, dh), lambda i: (i, 0))],
        out_specs=pl.BlockSpec(memory_space=pl.ANY),
        out_shape=jax.ShapeDtypeStruct((n_rows, dh), jnp.uint32),
        scratch_shapes=[pltpu.VMEM((MOE_BLOCK, dh), jnp.uint32),
                        pltpu.SemaphoreType.DMA(()), pltpu.SemaphoreType.DMA(())],
        compiler_params=_cparams(("arbitrary",)),
        name="moe_dispatch",
    )(dest_flat, fill, v2p)


def _expert_ffn_kernel(be_ref, nused_ref, xs_ref, wg_ref, wu_ref, wd_ref, ys_ref):
    used = pl.program_id(0) < nused_ref[0]

    @pl.when(used)
    def _():
        x = _unpack_rows_bf16(xs_ref[...])
        ys_ref[...] = _pack_rows(_swiglu(x, wg_ref[0], wu_ref[0], wd_ref[0]))

    @pl.when(jnp.logical_not(used))
    def _():
        ys_ref[...] = jnp.zeros_like(ys_ref)


def moe_expert_ffn(xs, blk_e, n_used, w_gate, w_up, w_down):
    n_rows, dh = xs.shape
    nb = n_rows // MOE_BLOCK
    E, D, de = w_gate.shape
    blk = lambda i, be, nu: (jnp.minimum(i, nu[0] - 1), 0)
    wmap = lambda i, be, nu: (be[i], 0, 0)
    return pl.pallas_call(
        _expert_ffn_kernel,
        grid_spec=pltpu.PrefetchScalarGridSpec(
            num_scalar_prefetch=2,
            grid=(nb,),
            in_specs=[pl.BlockSpec((MOE_BLOCK, dh), blk),
                      pl.BlockSpec((1, D, de), wmap),
                      pl.BlockSpec((1, D, de), wmap),
                      pl.BlockSpec((1, de, D), wmap)],
            out_specs=pl.BlockSpec((MOE_BLOCK, dh), lambda i, be, nu: (i, 0))),
        out_shape=jax.ShapeDtypeStruct((n_rows, dh), jnp.uint32),
        compiler_params=_cparams(("arbitrary",)),
        name="moe_expert_ffn",
    )(blk_e, n_used, xs, w_gate, w_up, w_down)


def _shared_combine_kernel(dest_ref, v_ref, x1_ref, w_ref, mod_ref, wg_ref, wu_ref, wd_ref, ys_hbm,
                           o_ref, ybuf, sem):
    tm, dh = v_ref.shape
    n_copies = tm * TOP_K

    def start(r, c):
        for kk in range(TOP_K):
            _row_copy(ys_hbm, dest_ref[r * DEST_SLOTS + kk], ybuf.at[kk], r, sem).start()
        return c

    lax.fori_loop(0, tm, start, 0)
    shared = _swiglu(_unpack_rows_bf16(v_ref[...]), wg_ref[...], wu_ref[...], wd_ref[...])
    pltpu.make_async_copy(ys_hbm.at[pl.ds(0, n_copies), :], ys_hbm.at[pl.ds(0, n_copies), :], sem).wait()

    lo, hi = shared[:, :dh], shared[:, dh:]
    for kk in range(TOP_K):
        w = w_ref[:, kk:kk + 1]
        ylo, yhi = _unpack_rows(ybuf[kk])
        lo = lo + w * ylo
        hi = hi + w * yhi
    g2 = mod_ref[0, 5:6, :]
    o_ref[:, :dh] = x1_ref[:, :dh] + g2[:, :dh] * lo
    o_ref[:, dh:] = x1_ref[:, dh:] + g2[:, dh:] * hi


def moe_shared_combine(v2p, x1, sel_w, dest_flat, ys, mod, ws_gate, ws_up, ws_down, seq, tm):
    T, dh = v2p.shape
    D = 2 * dh
    de = ws_gate.shape[1]
    per_b = seq // tm
    const = lambda i: (0, 0)
    resident = lambda shape: pl.BlockSpec(shape, const, pipeline_mode=pl.Buffered(1))
    return pl.pallas_call(
        _shared_combine_kernel,
        grid=(T // tm,),
        in_specs=[pl.BlockSpec((tm * DEST_SLOTS,), lambda i: (i,), memory_space=pltpu.SMEM),
                  pl.BlockSpec((tm, dh), lambda i: (i, 0)),
                  pl.BlockSpec((tm, D), lambda i: (i, 0)),
                  pl.BlockSpec((tm, LANES), lambda i: (i, 0)),
                  pl.BlockSpec((1,) + mod.shape[1:], lambda i: (i // per_b, 0, 0)),
                  resident((D, de)), resident((D, de)), resident((de, D)),
                  pl.BlockSpec(memory_space=pl.ANY)],
        out_specs=pl.BlockSpec((tm, D), lambda i: (i, 0)),
        out_shape=jax.ShapeDtypeStruct((T, D), F32),
        scratch_shapes=[pltpu.VMEM((TOP_K, tm, dh), jnp.uint32), pltpu.SemaphoreType.DMA(())],
        compiler_params=_cparams(("arbitrary",)),
        name="moe_shared_combine",
    )(dest_flat, v2p, x1, sel_w, mod, ws_gate, ws_up, ws_down, ys)


def _in_proj_layout(w_in, d_rnn, q_lora, kv_lora, d_model):
    o = [0]
    for wdt in (d_rnn, d_rnn, q_lora, kv_lora, QK_ROPE, d_model, d_model):
        o.append(o[-1] + wdt)
    xr, yg, cq, ckv, kr, g_rnn, g_mla = (w_in[:, o[i]:o[i + 1]] for i in range(7))
    pieces = [xr, yg, g_rnn, g_mla, cq, ckv, jnp.concatenate([kr, _rot_half(kr)], axis=1)]
    starts, pos = [], 0
    for p in pieces:
        assert pos % p.shape[1] == 0, "column block of the input projection is not aligned to its width"
        starts.append(pos // p.shape[1])
        pos += p.shape[1]
    n_pad = -(-pos // 512) * 512
    pieces.append(jnp.zeros((w_in.shape[0], n_pad - pos), w_in.dtype))
    return jnp.concatenate(pieces, axis=1).astype(BF16), starts


def _routing_tables(counts, sel_e, sel_r, n_tokens, n_experts):
    counts = counts.astype(jnp.int32)
    padded = (counts + MOE_BLOCK - 1) // MOE_BLOCK * MOE_BLOCK
    pad_end = jnp.cumsum(padded)
    pad_start = pad_end - padded
    nb = -(-(n_tokens * TOP_K) // MOE_BLOCK) + n_experts
    n_used = pad_end[-1] // MOE_BLOCK
    blk = jnp.arange(nb, dtype=jnp.int32)
    blk_e = jnp.sum((pad_end[None, :] <= (blk * MOE_BLOCK)[:, None]).astype(jnp.int32), axis=1)
    blk_e = jnp.minimum(blk_e, n_experts - 1)
    last_e = jnp.sum(jnp.where(blk == n_used - 1, blk_e, 0))
    blk_e = jnp.where(blk < n_used, blk_e, last_e)
    e_ids = jnp.arange(n_experts, dtype=jnp.int32)
    se, sr = sel_e[:, :DEST_SLOTS], sel_r[:, :DEST_SLOTS]
    dest = sr + jnp.sum(jnp.where(se[:, :, None] == e_ids, pad_start, 0), axis=-1)
    fill = jnp.concatenate([pad_start + counts, padded - counts, n_used.reshape(1)])
    return dest.reshape(-1), fill, blk_e, n_used.reshape(1), nb * MOE_BLOCK


def _layer(x2, c, positions, batch, seq, w_mod, b_mod, norm1, w_in, conv_w, conv_b, w_a, b_a, w_i, b_i,
           lru_lambda, q_a_norm, kv_a_norm, w_uq, w_ukv, q_norm, k_norm, w_rnn_out, w_mla_out, w_out,
           norm2, w_router, router_bias, w_gate, w_up, w_down, ws_gate, ws_up, ws_down):
    T, D = x2.shape
    d_rnn = conv_w.shape[1]
    q_lora, kv_lora = q_a_norm.shape[0], kv_a_norm.shape[0]
    n_experts = w_router.shape[1]
    assert w_uq.shape[2] == QK_HEAD and w_ukv.shape[2] == QK_NOPE + V_HEAD
    assert d_rnn == D and w_mla_out.shape[0] == D and n_experts <= LANES

    mod = adaln_mod(c, w_mod, b_mod)

    w_cat, (_, _, gr_blk, gm_blk, cq_blk, ckv_blk, kre_blk) = _in_proj_layout(w_in, d_rnn, q_lora, kv_lora, D)
    proj = in_projection(x2, mod, norm1, w_cat, seq, _tile(seq, 1024, SUBLANES), _tile(w_cat.shape[1], 1536))

    wai = jnp.concatenate([w_a, w_i], axis=-1).astype(BF16)
    y_rnn = rglru_branch(proj, conv_w, conv_b, wai, b_a, b_i, lru_lambda, batch, seq, _tile(seq, 256, 16))

    wq, wkv, gq, gk, freq, seg_q, seg_k = mla_weights(w_uq, w_ukv, q_norm, k_norm)
    q, k, vt = mla_prep(proj, positions, q_a_norm, kv_a_norm, wq, wkv, gq, gk, freq, seg_q, seg_k, batch, seq,
                        _tile(seq, 512, LANES), cq_blk, ckv_blk, kre_blk)
    n_heads = w_uq.shape[1]
    y_mla = mla_attention(q, k, vt, _tile(seq, 512, LANES), 4 if n_heads % 4 == 0 else 2).reshape(T, -1)

    w_router_pad = jnp.zeros((D, LANES), F32).at[:, :n_experts].set(w_router)
    w_router_hi = w_router_pad.astype(BF16)
    w_router_pad = jnp.stack([w_router_hi, (w_router_pad - w_router_hi.astype(F32)).astype(BF16)])
    router_bias_pad = jnp.zeros((1, LANES), F32).at[0, :n_experts].set(router_bias)
    x1, v2p, sel_e, sel_r, sel_w, cnt = merge_and_route(
        y_rnn, y_mla, proj, gr_blk, gm_blk, x2, mod, w_rnn_out.astype(BF16), w_mla_out.astype(BF16),
        w_out.astype(BF16), norm2, w_router_pad, router_bias_pad, n_experts, seq, _tile(seq, 256, 16))

    dest, fill, blk_e, n_used, n_rows = _routing_tables(cnt[0, :n_experts], sel_e, sel_r, T, n_experts)
    xs = moe_dispatch(v2p, dest, fill, n_rows, n_experts, _tile(seq, 512, 128))
    ys = moe_expert_ffn(xs, blk_e, n_used, w_gate.astype(BF16), w_up.astype(BF16), w_down.astype(BF16))
    return moe_shared_combine(v2p, x1, sel_w, dest, ys, mod, ws_gate.astype(BF16), ws_up.astype(BF16),
                              ws_down.astype(BF16), seq, _tile(seq, 256, 128))


def kernel(x, c, positions, w_mod, b_mod, norm1, w_in, conv_w, conv_b, w_a, b_a, w_i, b_i, lru_lambda,
           q_a_norm, kv_a_norm, w_uq, w_ukv, q_norm, k_norm, w_rnn_out, w_mla_out, w_out, norm2, w_router,
           router_bias, w_gate, w_up, w_down, ws_gate, ws_up, ws_down):
    B, S, D = x.shape
    x2 = x.reshape(B * S, D)
    per_layer = (w_mod, b_mod, norm1, w_in, conv_w, conv_b, w_a, b_a, w_i, b_i, lru_lambda, q_a_norm,
                 kv_a_norm, w_uq, w_ukv, q_norm, k_norm, w_rnn_out, w_mla_out, w_out, norm2, w_router,
                 router_bias, w_gate, w_up, w_down, ws_gate, ws_up, ws_down)
    for l in range(w_mod.shape[0]):
        x2 = _layer(x2, c, positions, B, S, *(w[l] for w in per_layer))
    return x2.reshape(B, S, D)
```

```python
import functools

import jax
import jax.numpy as jnp
from jax import lax
from jax.experimental import pallas as pl
from jax.experimental.pallas import tpu as pltpu

F32 = jnp.float32
BF16 = jnp.bfloat16

EPS = 1e-6
CHUNK = 64
LRU_C = 8.0
ROPE_THETA = 10000.0
TOP_K = 6
ROUTED_SCALE = 2.5
RNN_BLOCK_DIM = 128

LANES = 128
SUBLANES = 8
VMEM_LIMIT_BYTES = 56 * 1024 * 1024

MOE_BLOCK = 256
DEST_SLOTS = 8


def _cparams(semantics):
    return pltpu.CompilerParams(dimension_semantics=semantics, vmem_limit_bytes=VMEM_LIMIT_BYTES)


def _tile(n, target, align=LANES):
    if n <= target:
        return n
    t = (target // align) * align
    while t > align and n % t:
        t -= align
    assert n % t == 0, (n, target, align)
    return t


def _mod_kernel(c_ref, w_ref, b_ref, o_ref):
    c = c_ref[...]
    s = c * jax.nn.sigmoid(c)
    o_ref[...] = jnp.dot(s, w_ref[...], preferred_element_type=F32,
                         precision=lax.Precision.HIGHEST) + b_ref[...]


def adaln_mod(c, w_mod, b_mod):
    B, D = c.shape
    N = w_mod.shape[1]
    rows = -(-B // SUBLANES) * SUBLANES
    c_pad = jnp.zeros((rows, D), F32).at[:B].set(c)
    tn = _tile(N, 1024)
    out = pl.pallas_call(
        _mod_kernel,
        grid=(N // tn,),
        in_specs=[pl.BlockSpec((rows, D), lambda j: (0, 0)),
                  pl.BlockSpec((D, tn), lambda j: (0, j)),
                  pl.BlockSpec((1, tn), lambda j: (0, j))],
        out_specs=pl.BlockSpec((rows, tn), lambda j: (0, j)),
        out_shape=jax.ShapeDtypeStruct((rows, N), F32),
        compiler_params=_cparams(("arbitrary",)),
        name="adaln_mod",
    )(c_pad, w_mod, b_mod.reshape(1, N))
    return out[:B].reshape(B, N // D, D)


def _inproj_kernel(x_ref, mod_ref, g_ref, w_ref, o_ref):
    x = x_ref[...]
    ms = jnp.mean(x * x, axis=-1, keepdims=True)
    y = x * lax.rsqrt(ms + EPS) * g_ref[...]
    u = (y * (1.0 + mod_ref[0, 1:2, :]) + mod_ref[0, 0:1, :]).astype(BF16)
    o_ref[...] = jnp.dot(u, w_ref[...], preferred_element_type=F32).astype(o_ref.dtype)


def in_projection(x2, mod, norm1, w_cat, seq, tm):
    T, D = x2.shape
    N = w_cat.shape[1]
    per_b = seq // tm
    return pl.pallas_call(
        _inproj_kernel,
        grid=(T // tm,),
        in_specs=[pl.BlockSpec((tm, D), lambda i: (i, 0)),
                  pl.BlockSpec((1,) + mod.shape[1:], lambda i: (i // per_b, 0, 0)),
                  pl.BlockSpec((1, D), lambda i: (0, 0)),
                  pl.BlockSpec((D, N), lambda i: (0, 0), pipeline_mode=pl.Buffered(1))],
        out_specs=pl.BlockSpec((tm, N), lambda i: (i, 0)),
        out_shape=jax.ShapeDtypeStruct((T, N), BF16),
        compiler_params=_cparams(("arbitrary",)),
        name="in_projection",
    )(x2, mod, norm1.reshape(1, D), w_cat)


def _gelu_tanh(x):
    return 0.5 * x * (1.0 + jnp.tanh(0.7978845608028654 * (x + 0.044715 * (x * x * x))))


def _rglru_kernel(xr_ref, yg_ref, cw_ref, cb_ref, wai_ref, ba_ref, bi_ref, lam_ref, o_ref,
                  xbuf, xcbuf, abuf, bbuf, hc, *, conv_width):
    ts, dr = xcbuf.shape
    halo = SUBLANES
    nblk = dr // RNN_BLOCK_DIM

    @pl.when(pl.program_id(1) == 0)
    def _():
        xbuf[0:halo, :] = jnp.zeros((halo, dr), F32)
        hc[...] = jnp.zeros_like(hc)

    xbuf[halo:halo + ts, :] = xr_ref[...].astype(F32)
    xc = cb_ref[...] + xbuf[halo:halo + ts, :] * cw_ref[conv_width - 1:conv_width, :]
    for j in range(conv_width - 1):
        off = halo - (conv_width - 1) + j
        xc = xc + xbuf[off:off + ts, :] * cw_ref[j:j + 1, :]
    xcbuf[...] = xc
    xbuf[0:halo, :] = xbuf[ts:ts + halo, :]

    z = -lam_ref[...]
    sp = jnp.maximum(z, 0.0) + jnp.log1p(jnp.exp(-jnp.abs(z)))
    for k in range(nblk):
        cols = slice(k * RNN_BLOCK_DIM, (k + 1) * RNN_BLOCK_DIM)
        xk = xcbuf[:, cols]
        g = jnp.dot(xk.astype(BF16), wai_ref[k], preferred_element_type=F32)
        r = jax.nn.sigmoid(g[:, :RNN_BLOCK_DIM] + ba_ref[:, cols])
        ig = jax.nn.sigmoid(g[:, RNN_BLOCK_DIM:] + bi_ref[:, cols])
        a = jnp.exp((-LRU_C) * r * sp[:, cols])
        abuf[:, cols] = a
        bbuf[:, cols] = jnp.sqrt(1.0 - a * a) * (ig * xk)

    row = lax.broadcasted_iota(jnp.int32, (SUBLANES, dr), 0)
    rows2 = 2 * SUBLANES

    def scan8(a, b, h):
        for s in (1, 2, 4):
            m = row >= s
            a_s = jnp.where(m, pltpu.roll(a, s, 0), 1.0)
            b_s = jnp.where(m, pltpu.roll(b, s, 0), 0.0)
            b = a * b_s + b
            a = a * a_s
        hh = a * h + b
        return hh, jnp.broadcast_to(hh[SUBLANES - 1:SUBLANES, :], (SUBLANES, dr))

    def body(c, h):
        r0 = pl.multiple_of(c * rows2, rows2)
        a16 = abuf[pl.ds(r0, rows2), :]
        b16 = bbuf[pl.ds(r0, rows2), :]
        h0, h = scan8(a16[:SUBLANES], b16[:SUBLANES], h)
        h1, h = scan8(a16[SUBLANES:], b16[SUBLANES:], h)
        hs = jnp.concatenate([h0, h1], axis=0)
        yg = yg_ref[pl.ds(r0, rows2), :].astype(F32)
        o_ref[pl.ds(r0, rows2), :] = (hs * _gelu_tanh(yg)).astype(o_ref.dtype)
        return h

    hc[...] = lax.fori_loop(0, ts // rows2, body, hc[...])


def rglru_branch(proj, conv_w, conv_b, wai, b_a, b_i, lam, batch, seq, ts):
    T = proj.shape[0]
    W, dr = conv_w.shape
    nblk = dr // RNN_BLOCK_DIM
    per_b = seq // ts
    vec = lambda: pl.BlockSpec((1, dr), lambda b, i: (0, 0))
    return pl.pallas_call(
        functools.partial(_rglru_kernel, conv_width=W),
        grid=(batch, per_b),
        in_specs=[pl.BlockSpec((ts, dr), lambda b, i: (b * per_b + i, 0)),
                  pl.BlockSpec((ts, dr), lambda b, i: (b * per_b + i, 1)),
                  pl.BlockSpec((W, dr), lambda b, i: (0, 0)),
                  vec(),
                  pl.BlockSpec((nblk, RNN_BLOCK_DIM, 2 * RNN_BLOCK_DIM), lambda b, i: (0, 0, 0)),
                  vec(), vec(), vec()],
        out_specs=pl.BlockSpec((ts, dr), lambda b, i: (b * per_b + i, 0)),
        out_shape=jax.ShapeDtypeStruct((T, dr), BF16),
        scratch_shapes=[pltpu.VMEM((ts + SUBLANES, dr), F32),
                        pltpu.VMEM((ts, dr), F32),
                        pltpu.VMEM((ts, dr), F32),
                        pltpu.VMEM((ts, dr), F32),
                        pltpu.VMEM((SUBLANES, dr), F32)],
        compiler_params=_cparams(("arbitrary", "arbitrary")),
        name="rglru_branch",
    )(proj, proj, conv_w, conv_b.reshape(1, dr), wai, b_a.reshape(1, dr), b_i.reshape(1, dr),
      lam.reshape(1, dr))


QK_NOPE = 128
QK_ROPE = 64
V_HEAD = 128
QK_HEAD = QK_NOPE + QK_ROPE


def _rot_half(w):
    half = QK_ROPE // 2
    return jnp.concatenate([-w[..., half:], w[..., :half]], axis=-1)


def mla_weights(w_uq, w_ukv, q_norm, k_norm):
    r, H, _ = w_uq.shape
    half = QK_ROPE // 2
    pair = lambda w: w.reshape(w.shape[0], H // 2, 2 * w.shape[2])
    rope_w = w_uq[..., QK_NOPE:]
    wq = jnp.concatenate([pair(w_uq[..., :QK_NOPE]), pair(rope_w), pair(_rot_half(rope_w))], axis=-1)
    wkv = jnp.concatenate([pair(w_ukv[..., :QK_NOPE]), pair(w_ukv[..., QK_NOPE:])], axis=-1)
    wq = wq.transpose(1, 0, 2).astype(BF16)
    wkv = wkv.transpose(1, 0, 2).astype(BF16)

    def gains(g):
        gr = g[QK_NOPE:]
        swapped = jnp.concatenate([gr[half:], gr[:half]])
        return jnp.stack([g[:QK_NOPE], jnp.tile(gr, 2), jnp.tile(swapped, 2)]).astype(F32)

    inv_freq = ROPE_THETA ** (-jnp.arange(0, QK_ROPE, 2, dtype=F32) / QK_ROPE)
    freq = jnp.tile(inv_freq, 2 * LANES // QK_ROPE).reshape(1, LANES)

    rows = jnp.arange(3 * LANES)[:, None]
    cols = jnp.arange(2 * LANES)[None, :]
    head_b = cols >= LANES
    in_nope = (rows // LANES) == jnp.where(head_b, 1, 0)
    in_rope = (rows >= 2 * LANES) & (((rows - 2 * LANES) // QK_ROPE) == jnp.where(head_b, 1, 0))
    seg_q = (in_nope | in_rope).astype(BF16)
    seg_k = in_nope[:2 * LANES].astype(BF16)
    return wq, wkv, gains(q_norm), gains(k_norm), freq, seg_q, seg_k


def _mla_prep_kernel(cq_ref, ckv_ref, kre_ref, pos_ref, qan_ref, kvan_ref, wq_ref, wkv_ref,
                     gq_ref, gk_ref, freq_ref, segq_ref, segk_ref, q_ref, k_ref, vt_ref, cqn, ckvn):
    tm = cq_ref.shape[0]
    n_pairs = wq_ref.shape[0]

    def latent_norm(ref, g_ref):
        c = ref[...].astype(F32)
        ms = jnp.mean(c * c, axis=-1, keepdims=True)
        return (c * lax.rsqrt(ms + EPS) * g_ref[...]).astype(BF16)

    cqn[...] = latent_norm(cq_ref, qan_ref)
    ckvn[...] = latent_norm(ckv_ref, kvan_ref)

    lane = lax.broadcasted_iota(jnp.int32, (tm, LANES), 1)
    lo = lane < QK_ROPE
    ang = pos_ref[0].astype(F32) * freq_ref[...]
    cos_t, sin_t = jnp.cos(ang), jnp.sin(ang)
    q_cos = gq_ref[1:2, :] * cos_t
    q_sin = gq_ref[2:3, :] * sin_t
    q_scale = (QK_HEAD ** -0.5) * 1.4426950408889634

    kre = kre_ref[...].astype(F32)
    ss_kr = jnp.dot(jnp.where(lo, kre * kre, 0.0).astype(BF16), segk_ref[0:LANES, 0:LANES],
                    preferred_element_type=F32)
    z = kre * jnp.where(lo, gk_ref[1:2, :] * cos_t, gk_ref[2:3, :] * sin_t)
    k_rope = z + pltpu.roll(z, QK_ROPE, 1)
    k_rope_a = jnp.where(lo, k_rope, 0.0)
    k_rope_b = jnp.where(lo, 0.0, k_rope)

    def seg_sums(pieces, seg_ref):
        sq = jnp.concatenate([(p * p).astype(BF16) for p in pieces], axis=1)
        return jnp.dot(sq, seg_ref[...], preferred_element_type=F32)

    def pair(p, carry):
        qp = jnp.dot(cqn[...], wq_ref[p], preferred_element_type=F32)
        qa, qb, rope_raw, rot_raw = (qp[:, j * LANES:(j + 1) * LANES] for j in range(4))
        ss = seg_sums([qa, qb, rope_raw], segq_ref)
        inv = lax.rsqrt(ss * (1.0 / QK_HEAD) + EPS) * q_scale
        rope = rope_raw * q_cos + rot_raw * q_sin
        for j, qn in enumerate((qa, qb)):
            inv_j = inv[:, j * LANES:(j + 1) * LANES]
            q_ref[0, 2 * p + j] = jnp.concatenate(
                [qn * gq_ref[0:1, :] * inv_j, rope * inv_j], axis=1).astype(q_ref.dtype)

        kvp = jnp.dot(ckvn[...], wkv_ref[p], preferred_element_type=F32)
        ka, kb, va, vb = (kvp[:, j * LANES:(j + 1) * LANES] for j in range(4))
        ssk = seg_sums([ka, kb], segk_ref)
        for j, (kn, kr_j, vj) in enumerate(((ka, k_rope_a, va), (kb, k_rope_b, vb))):
            inv_j = lax.rsqrt((ssk[:, j * LANES:(j + 1) * LANES] + ss_kr) * (1.0 / QK_HEAD) + EPS)
            k_ref[0, 2 * p + j] = jnp.concatenate(
                [kn * gk_ref[0:1, :] * inv_j, kr_j * inv_j], axis=1).astype(k_ref.dtype)
            vt_ref[0, 2 * p + j] = vj.T.astype(vt_ref.dtype)
        return carry

    lax.fori_loop(0, n_pairs, pair, 0)


def mla_prep(proj, positions, q_a_norm, kv_a_norm, wq, wkv, gq, gk, freq, seg_q, seg_k, batch, seq, tm,
             cq_blk, ckv_blk, kre_blk):
    n_pairs, q_lora, _ = wq.shape
    kv_lora = wkv.shape[1]
    H = 2 * n_pairs
    per_b = seq // tm
    dq = QK_NOPE + LANES
    row = lambda blk: (lambda b, i: (b * per_b + i, blk))
    const2 = lambda b, i: (0, 0)
    const3 = lambda b, i: (0, 0, 0)
    out_map = lambda b, i: (b, 0, i, 0)
    return pl.pallas_call(
        _mla_prep_kernel,
        grid=(batch, per_b),
        in_specs=[pl.BlockSpec((tm, q_lora), row(cq_blk)),
                  pl.BlockSpec((tm, kv_lora), row(ckv_blk)),
                  pl.BlockSpec((tm, LANES), row(kre_blk)),
                  pl.BlockSpec((1, tm, 1), lambda b, i: (b, i, 0)),
                  pl.BlockSpec((1, q_lora), const2),
                  pl.BlockSpec((1, kv_lora), const2),
                  pl.BlockSpec(wq.shape, const3),
                  pl.BlockSpec(wkv.shape, const3),
                  pl.BlockSpec(gq.shape, const2),
                  pl.BlockSpec(gk.shape, const2),
                  pl.BlockSpec((1, LANES), const2),
                  pl.BlockSpec(seg_q.shape, const2),
                  pl.BlockSpec(seg_k.shape, const2)],
        out_specs=[pl.BlockSpec((1, H, tm, dq), out_map),
                   pl.BlockSpec((1, H, tm, dq), out_map),
                   pl.BlockSpec((1, H, V_HEAD, tm), lambda b, i: (b, 0, 0, i))],
        out_shape=[jax.ShapeDtypeStruct((batch, H, seq, dq), BF16),
                   jax.ShapeDtypeStruct((batch, H, seq, dq), BF16),
                   jax.ShapeDtypeStruct((batch, H, V_HEAD, seq), BF16)],
        scratch_shapes=[pltpu.VMEM((tm, q_lora), BF16), pltpu.VMEM((tm, kv_lora), BF16)],
        compiler_params=_cparams(("arbitrary", "arbitrary")),
        name="mla_prep",
    )(proj, proj, proj, positions.reshape(batch, seq, 1), q_a_norm.reshape(1, q_lora),
      kv_a_norm.reshape(1, kv_lora), wq, wkv, gq, gk, freq, seg_q, seg_k)


def _attn_kernel(q_ref, k_ref, vt_ref, o_ref, qt_ref, s0_ref, s1_ref, m_ref, l_ref, acc_ref):
    nh, tq = q_ref.shape[1], q_ref.shape[2]
    dv = vt_ref.shape[2]
    i = pl.program_id(2)
    m_ref[...] = jnp.full(m_ref.shape, -jnp.inf, F32)
    l_ref[...] = jnp.zeros(l_ref.shape, F32)
    acc_ref[...] = jnp.zeros(acc_ref.shape, F32)
    for h in range(nh):
        qt_ref[h] = q_ref[0, h].astype(F32).T.astype(qt_ref.dtype)

    def scores(j, s_ref):
        start = pl.multiple_of(j * tq, tq)
        for h in range(nh):
            s_ref[h] = jnp.dot(k_ref[0, h, pl.ds(start, tq), :], qt_ref[h], preferred_element_type=F32)

    def consume(j, s_ref, masked):
        start = pl.multiple_of(j * tq, tq)
        for h in range(nh):
            s = s_ref[h]
            vt = vt_ref[0, h, :, pl.ds(start, tq)]
            if masked:
                kc = lax.broadcasted_iota(jnp.int32, s.shape, 0) // CHUNK
                qc = lax.broadcasted_iota(jnp.int32, s.shape, 1) // CHUNK
                s = jnp.where(kc <= qc, s, -jnp.inf)
            m_prev = m_ref[h]
            m_new = jnp.maximum(m_prev, jnp.max(s, axis=0, keepdims=True))
            alpha = jnp.exp2(m_prev - m_new)
            p = jnp.exp2(s - m_new)
            l_ref[h] = alpha * l_ref[h] + jnp.sum(p, axis=0, keepdims=True)
            acc_ref[h] = alpha * acc_ref[h] + jnp.dot(vt, p.astype(vt.dtype), preferred_element_type=F32)
            m_ref[h] = m_new

    scores(0, s0_ref)

    def two_blocks(jj, carry):
        scores(2 * jj + 1, s1_ref)
        consume(2 * jj, s0_ref, False)
        scores(2 * jj + 2, s0_ref)
        consume(2 * jj + 1, s1_ref, False)
        return carry

    lax.fori_loop(0, i // 2, two_blocks, 0)

    @pl.when(i % 2 == 0)
    def _():
        consume(i, s0_ref, True)

    @pl.when(i % 2 == 1)
    def _():
        scores(i, s1_ref)
        consume(i - 1, s0_ref, False)
        consume(i, s1_ref, True)

    for h in range(nh):
        o_ref[0, :, h * dv:(h + 1) * dv] = (acc_ref[h] / l_ref[h]).T.astype(o_ref.dtype)


def mla_attention(q, k, vt, tq, nh):
    B, H, S, dq = q.shape
    dv = vt.shape[2]
    assert tq % CHUNK == 0 and S % tq == 0 and H % nh == 0
    head_map = lambda b, g, i: (b, g, 0, 0)
    return pl.pallas_call(
        _attn_kernel,
        grid=(B, H // nh, S // tq),
        in_specs=[pl.BlockSpec((1, nh, tq, dq), lambda b, g, i: (b, g, i, 0)),
                  pl.BlockSpec((1, nh, S, dq), head_map, pipeline_mode=pl.Buffered(1)),
                  pl.BlockSpec((1, nh, dv, S), head_map, pipeline_mode=pl.Buffered(1))],
        out_specs=pl.BlockSpec((1, tq, nh * dv), lambda b, g, i: (b, i, g)),
        out_shape=jax.ShapeDtypeStruct((B, S, H * dv), BF16),
        scratch_shapes=[pltpu.VMEM((nh, dq, tq), BF16),
                        pltpu.VMEM((nh, tq, tq), F32), pltpu.VMEM((nh, tq, tq), F32),
                        pltpu.VMEM((nh, 1, tq), F32), pltpu.VMEM((nh, 1, tq), F32),
                        pltpu.VMEM((nh, dv, tq), F32)],
        compiler_params=_cparams(("arbitrary", "arbitrary", "arbitrary")),
        name="mla_attention",
    )(q, k, vt)


def _merge_route_kernel(yr_ref, ym_ref, gr_ref, gm_ref, x_ref, mod_ref, wr_ref, wm_ref, wo_ref,
                        n2_ref, wrt_ref, rb_ref,
                        x1_ref, v2_ref, sel_e_ref, sel_r_ref, sel_w_ref, cnt_ref, cnt_acc,
                        *, n_experts):
    tm = x_ref.shape[0]

    @pl.when(pl.program_id(0) == 0)
    def _():
        cnt_acc[...] = jnp.zeros_like(cnt_acc)

    a = jnp.dot(yr_ref[...], wr_ref[...], preferred_element_type=F32)
    b = jnp.dot(ym_ref[...], wm_ref[...], preferred_element_type=F32)
    merged = (jax.nn.sigmoid(gr_ref[...].astype(F32)) * a
              + jax.nn.sigmoid(gm_ref[...].astype(F32)) * b)
    c = jnp.dot(merged.astype(BF16), wo_ref[...], preferred_element_type=F32)
    x1 = x_ref[...] + mod_ref[0, 2:3, :] * c
    x1_ref[...] = x1

    ms = jnp.mean(x1 * x1, axis=-1, keepdims=True)
    v = x1 * lax.rsqrt(ms + EPS) * n2_ref[...]
    v = v * (1.0 + mod_ref[0, 4:5, :]) + mod_ref[0, 3:4, :]
    v2_ref[...] = _pack_rows(v)

    v_hi = v.astype(BF16)
    v_lo = (v - v_hi.astype(F32)).astype(BF16)
    logits = (jnp.dot(v_hi, wrt_ref[0], preferred_element_type=F32)
              + jnp.dot(v_hi, wrt_ref[1], preferred_element_type=F32)
              + jnp.dot(v_lo, wrt_ref[0], preferred_element_type=F32))
    scores = jax.nn.sigmoid(logits)
    lane = lax.broadcasted_iota(jnp.int32, (tm, LANES), 1)
    biased = jnp.where(lane < n_experts, scores + rb_ref[...], -jnp.inf)
    picked = jnp.zeros((tm, LANES), jnp.bool_)
    idxs, tops = [], []
    for _ in range(TOP_K):
        mx = jnp.max(biased, axis=-1, keepdims=True)
        idx = jnp.min(jnp.where(biased == mx, lane, LANES), axis=-1, keepdims=True)
        hit = lane == idx
        idxs.append(idx)
        tops.append(jnp.sum(jnp.where(hit, scores, 0.0), axis=-1, keepdims=True))
        biased = jnp.where(hit, -jnp.inf, biased)
        picked = jnp.logical_or(picked, hit)
    denom = tops[0]
    for t in tops[1:]:
        denom = denom + t

    onehot = jnp.where(picked, 1.0, 0.0)
    r_i = lax.broadcasted_iota(jnp.int32, (tm, tm), 0)
    c_i = lax.broadcasted_iota(jnp.int32, (tm, tm), 1)
    tri = jnp.where(c_i < r_i, 1.0, 0.0).astype(BF16)
    rank = jnp.dot(tri, onehot.astype(BF16), preferred_element_type=F32) + cnt_acc[0:1, :]
    cnt_acc[0:1, :] = cnt_acc[0:1, :] + jnp.sum(onehot, axis=0, keepdims=True)
    cnt_ref[...] = cnt_acc[...]

    sel_e = jnp.zeros((tm, LANES), jnp.int32)
    sel_r = jnp.zeros((tm, LANES), jnp.int32)
    sel_w = jnp.zeros((tm, LANES), F32)
    for kk in range(TOP_K):
        here = lane == kk
        rk = jnp.sum(jnp.where(lane == idxs[kk], rank, 0.0), axis=-1, keepdims=True)
        sel_e = jnp.where(here, idxs[kk], sel_e)
        sel_r = jnp.where(here, rk.astype(jnp.int32), sel_r)
        sel_w = jnp.where(here, tops[kk] / denom * ROUTED_SCALE, sel_w)
    sel_e_ref[...] = sel_e
    sel_r_ref[...] = sel_r
    sel_w_ref[...] = sel_w


def merge_and_route(y_rnn, y_mla, proj, gr_blk, gm_blk, x2, mod, w_rnn_out, w_mla_out, w_out, norm2,
                    w_router_pad, router_bias_pad, n_experts, seq, tm):
    T, D = x2.shape
    per_b = seq // tm
    row = lambda blk: (lambda i: (i, blk))
    const = lambda i: (0, 0)
    resident = lambda shape: pl.BlockSpec(shape, const, pipeline_mode=pl.Buffered(1))
    lanes_out = lambda: pl.BlockSpec((tm, LANES), row(0))
    return pl.pallas_call(
        functools.partial(_merge_route_kernel, n_experts=n_experts),
        grid=(T // tm,),
        in_specs=[pl.BlockSpec((tm, D), row(0)),
                  pl.BlockSpec((tm, D), row(0)),
                  pl.BlockSpec((tm, D), row(gr_blk)),
                  pl.BlockSpec((tm, D), row(gm_blk)),
                  pl.BlockSpec((tm, D), row(0)),
                  pl.BlockSpec((1,) + mod.shape[1:], lambda i: (i // per_b, 0, 0)),
                  resident((D, D)), resident((D, D)), resident((D, D)),
                  pl.BlockSpec((1, D), const),
                  pl.BlockSpec((2, D, LANES), lambda i: (0, 0, 0)),
                  pl.BlockSpec((1, LANES), const)],
        out_specs=[pl.BlockSpec((tm, D), row(0)),
                   pl.BlockSpec((tm, D // 2), row(0)),
                   lanes_out(), lanes_out(), lanes_out(),
                   pl.BlockSpec((SUBLANES, LANES), const)],
        out_shape=[jax.ShapeDtypeStruct((T, D), F32),
                   jax.ShapeDtypeStruct((T, D // 2), jnp.uint32),
                   jax.ShapeDtypeStruct((T, LANES), jnp.int32),
                   jax.ShapeDtypeStruct((T, LANES), jnp.int32),
                   jax.ShapeDtypeStruct((T, LANES), F32),
                   jax.ShapeDtypeStruct((SUBLANES, LANES), F32)],
        scratch_shapes=[pltpu.VMEM((SUBLANES, LANES), F32)],
        compiler_params=_cparams(("arbitrary",)),
        name="merge_and_route",
    )(y_rnn, y_mla, proj, proj, x2, mod, w_rnn_out, w_mla_out, w_out, norm2.reshape(1, D),
      w_router_pad, router_bias_pad)


def _pack_rows(x):
    half = x.shape[1] // 2
    return pltpu.pack_elementwise([x[:, :half], x[:, half:]], packed_dtype=BF16)


def _unpack_rows(p):
    lo = pltpu.unpack_elementwise(p, index=0, packed_dtype=BF16, unpacked_dtype=F32)
    hi = pltpu.unpack_elementwise(p, index=1, packed_dtype=BF16, unpacked_dtype=F32)
    return lo, hi


def _unpack_rows_bf16(p):
    lo, hi = _unpack_rows(p)
    return jnp.concatenate([lo.astype(BF16), hi.astype(BF16)], axis=1)


def _swiglu(x, wg, wu, wd):
    g = jnp.dot(x, wg, preferred_element_type=F32)
    u = jnp.dot(x, wu, preferred_element_type=F32)
    h = (g * jax.nn.sigmoid(g)) * u
    return jnp.dot(h.astype(BF16), wd, preferred_element_type=F32)


def _row_copy(src, src_row, dst, dst_row, sem):
    return pltpu.make_async_copy(src.at[pl.ds(src_row, 1), :], dst.at[pl.ds(dst_row, 1), :], sem)


def _dispatch_kernel(dest_ref, fill_ref, v_ref, xs_hbm, zrow, sem, zsem, *, n_experts):
    tm = v_ref.shape[0]
    n_copies = tm * TOP_K

    def start(r, c):
        for kk in range(TOP_K):
            _row_copy(v_ref, r, xs_hbm, dest_ref[r * DEST_SLOTS + kk], sem).start(priority=kk % 2)
        return c

    lax.fori_loop(0, tm, start, 0)

    @pl.when(pl.program_id(0) == 0)
    def _():
        zrow[...] = jnp.zeros_like(zrow)

        def per_expert(e, c):
            first, count = fill_ref[e], fill_ref[n_experts + e]

            def one(r, c2):
                _row_copy(zrow, 0, xs_hbm, first + r, zsem).start()
                return c2

            def one_wait(r, c2):
                _row_copy(zrow, 0, xs_hbm, first, zsem).wait()
                return c2

            lax.fori_loop(0, count, one, 0)
            lax.fori_loop(0, count, one_wait, 0)
            return c

        lax.fori_loop(0, n_experts, per_expert, 0)

        def block_copy(b):
            return pltpu.make_async_copy(zrow, xs_hbm.at[pl.ds(b * MOE_BLOCK, MOE_BLOCK), :], zsem)

        def tail(b, c):
            block_copy(b).start()
            block_copy(b).wait()
            return c

        lax.fori_loop(fill_ref[2 * n_experts], xs_hbm.shape[0] // MOE_BLOCK, tail, 0)

    pltpu.make_async_copy(xs_hbm.at[pl.ds(0, n_copies), :], xs_hbm.at[pl.ds(0, n_copies), :], sem).wait()


def moe_dispatch(v2p, dest_flat, fill, n_rows, n_experts, tm):
    T, dh = v2p.shape
    return pl.pallas_call(
        functools.partial(_dispatch_kernel, n_experts=n_experts),
        grid=(T // tm,),
        in_specs=[pl.BlockSpec((tm * DEST_SLOTS,), lambda i: (i,), memory_space=pltpu.SMEM),
                  pl.BlockSpec(memory_space=pltpu.SMEM),
                  pl.BlockSpec((tm, dh), lambda i: (i, 0))],
        out_specs=pl.BlockSpec(memory_space=pl.ANY),
        out_shape=jax.ShapeDtypeStruct((n_rows, dh), jnp.uint32),
        scratch_shapes=[pltpu.VMEM((MOE_BLOCK, dh), jnp.uint32),
                        pltpu.SemaphoreType.DMA(()), pltpu.SemaphoreType.DMA(())],
        compiler_params=_cparams(("arbitrary",)),
        name="moe_dispatch",
    )(dest_flat, fill, v2p)


def _expert_ffn_kernel(be_ref, nused_ref, xs_ref, wg_ref, wu_ref, wd_ref, ys_ref):
    used = pl.program_id(0) < nused_ref[0]

    @pl.when(used)
    def _():
        x = _unpack_rows_bf16(xs_ref[...])
        ys_ref[...] = _pack_rows(_swiglu(x, wg_ref[0], wu_ref[0], wd_ref[0]))

    @pl.when(jnp.logical_not(used))
    def _():
        ys_ref[...] = jnp.zeros_like(ys_ref)


def moe_expert_ffn(xs, blk_e, n_used, w_gate, w_up, w_down):
    n_rows, dh = xs.shape
    nb = n_rows // MOE_BLOCK
    E, D, de = w_gate.shape
    blk = lambda i, be, nu: (jnp.minimum(i, nu[0] - 1), 0)
    wmap = lambda i, be, nu: (be[i], 0, 0)
    return pl.pallas_call(
        _expert_ffn_kernel,
        grid_spec=pltpu.PrefetchScalarGridSpec(
            num_scalar_prefetch=2,
            grid=(nb,),
            in_specs=[pl.BlockSpec((MOE_BLOCK, dh), blk),
                      pl.BlockSpec((1, D, de), wmap),
                      pl.BlockSpec((1, D, de), wmap),
                      pl.BlockSpec((1, de, D), wmap)],
            out_specs=pl.BlockSpec((MOE_BLOCK, dh), lambda i, be, nu: (i, 0))),
        out_shape=jax.ShapeDtypeStruct((n_rows, dh), jnp.uint32),
        compiler_params=_cparams(("arbitrary",)),
        name="moe_expert_ffn",
    )(blk_e, n_used, xs, w_gate, w_up, w_down)


def _shared_combine_kernel(dest_ref, v_ref, x1_ref, w_ref, mod_ref, wg_ref, wu_ref, wd_ref, ys_hbm,
                           o_ref, ybuf, sem):
    tm, dh = v_ref.shape
    n_copies = tm * TOP_K

    def start(r, c):
        for kk in range(TOP_K):
            _row_copy(ys_hbm, dest_ref[r * DEST_SLOTS + kk], ybuf.at[kk], r, sem).start(priority=kk % 2)
        return c

    lax.fori_loop(0, tm, start, 0)
    shared = _swiglu(_unpack_rows_bf16(v_ref[...]), wg_ref[...], wu_ref[...], wd_ref[...])
    pltpu.make_async_copy(ys_hbm.at[pl.ds(0, n_copies), :], ys_hbm.at[pl.ds(0, n_copies), :], sem).wait()

    lo, hi = shared[:, :dh], shared[:, dh:]
    for kk in range(TOP_K):
        w = w_ref[:, kk:kk + 1]
        ylo, yhi = _unpack_rows(ybuf[kk])
        lo = lo + w * ylo
        hi = hi + w * yhi
    g2 = mod_ref[0, 5:6, :]
    o_ref[:, :dh] = x1_ref[:, :dh] + g2[:, :dh] * lo
    o_ref[:, dh:] = x1_ref[:, dh:] + g2[:, dh:] * hi


def moe_shared_combine(v2p, x1, sel_w, dest_flat, ys, mod, ws_gate, ws_up, ws_down, seq, tm):
    T, dh = v2p.shape
    D = 2 * dh
    de = ws_gate.shape[1]
    per_b = seq // tm
    const = lambda i: (0, 0)
    resident = lambda shape: pl.BlockSpec(shape, const, pipeline_mode=pl.Buffered(1))
    return pl.pallas_call(
        _shared_combine_kernel,
        grid=(T // tm,),
        in_specs=[pl.BlockSpec((tm * DEST_SLOTS,), lambda i: (i,), memory_space=pltpu.SMEM),
                  pl.BlockSpec((tm, dh), lambda i: (i, 0)),
                  pl.BlockSpec((tm, D), lambda i: (i, 0)),
                  pl.BlockSpec((tm, LANES), lambda i: (i, 0)),
                  pl.BlockSpec((1,) + mod.shape[1:], lambda i: (i // per_b, 0, 0)),
                  resident((D, de)), resident((D, de)), resident((de, D)),
                  pl.BlockSpec(memory_space=pl.ANY)],
        out_specs=pl.BlockSpec((tm, D), lambda i: (i, 0)),
        out_shape=jax.ShapeDtypeStruct((T, D), F32),
        scratch_shapes=[pltpu.VMEM((TOP_K, tm, dh), jnp.uint32), pltpu.SemaphoreType.DMA(())],
        compiler_params=_cparams(("arbitrary",)),
        name="moe_shared_combine",
    )(dest_flat, v2p, x1, sel_w, mod, ws_gate, ws_up, ws_down, ys)


def _in_proj_layout(w_in, d_rnn, q_lora, kv_lora, d_model):
    o = [0]
    for wdt in (d_rnn, d_rnn, q_lora, kv_lora, QK_ROPE, d_model, d_model):
        o.append(o[-1] + wdt)
    xr, yg, cq, ckv, kr, g_rnn, g_mla = (w_in[:, o[i]:o[i + 1]] for i in range(7))
    pieces = [xr, yg, g_rnn, g_mla, cq, ckv, jnp.concatenate([kr, _rot_half(kr)], axis=1)]
    starts, pos = [], 0
    for p in pieces:
        assert pos % p.shape[1] == 0, "column block of the input projection is not aligned to its width"
        starts.append(pos // p.shape[1])
        pos += p.shape[1]
    n_pad = -(-pos // 512) * 512
    pieces.append(jnp.zeros((w_in.shape[0], n_pad - pos), w_in.dtype))
    return jnp.concatenate(pieces, axis=1).astype(BF16), starts


def _routing_tables(counts, sel_e, sel_r, n_tokens, n_experts):
    counts = counts.astype(jnp.int32)
    padded = (counts + MOE_BLOCK - 1) // MOE_BLOCK * MOE_BLOCK
    pad_end = jnp.cumsum(padded)
    pad_start = pad_end - padded
    nb = -(-(n_tokens * TOP_K) // MOE_BLOCK) + n_experts
    n_used = pad_end[-1] // MOE_BLOCK
    blk = jnp.arange(nb, dtype=jnp.int32)
    blk_e = jnp.sum((pad_end[None, :] <= (blk * MOE_BLOCK)[:, None]).astype(jnp.int32), axis=1)
    blk_e = jnp.minimum(blk_e, n_experts - 1)
    last_e = jnp.sum(jnp.where(blk == n_used - 1, blk_e, 0))
    blk_e = jnp.where(blk < n_used, blk_e, last_e)
    e_ids = jnp.arange(n_experts, dtype=jnp.int32)
    se, sr = sel_e[:, :DEST_SLOTS], sel_r[:, :DEST_SLOTS]
    dest = sr + jnp.sum(jnp.where(se[:, :, None] == e_ids, pad_start, 0), axis=-1)
    fill = jnp.concatenate([pad_start + counts, padded - counts, n_used.reshape(1)])
    return dest.reshape(-1), fill, blk_e, n_used.reshape(1), nb * MOE_BLOCK


def _layer(x2, c, positions, batch, seq, w_mod, b_mod, norm1, w_in, conv_w, conv_b, w_a, b_a, w_i, b_i,
           lru_lambda, q_a_norm, kv_a_norm, w_uq, w_ukv, q_norm, k_norm, w_rnn_out, w_mla_out, w_out,
           norm2, w_router, router_bias, w_gate, w_up, w_down, ws_gate, ws_up, ws_down):
    T, D = x2.shape
    d_rnn = conv_w.shape[1]
    q_lora, kv_lora = q_a_norm.shape[0], kv_a_norm.shape[0]
    n_experts = w_router.shape[1]
    assert w_uq.shape[2] == QK_HEAD and w_ukv.shape[2] == QK_NOPE + V_HEAD
    assert d_rnn == D and w_mla_out.shape[0] == D and n_experts <= LANES

    mod = adaln_mod(c, w_mod, b_mod)

    w_cat, (_, _, gr_blk, gm_blk, cq_blk, ckv_blk, kre_blk) = _in_proj_layout(w_in, d_rnn, q_lora, kv_lora, D)
    proj = in_projection(x2, mod, norm1, w_cat, seq, _tile(seq, 256, 16))

    wai = jnp.concatenate([w_a, w_i], axis=-1).astype(BF16)
    y_rnn = rglru_branch(proj, conv_w, conv_b, wai, b_a, b_i, lru_lambda, batch, seq, _tile(seq, 256, 16))

    wq, wkv, gq, gk, freq, seg_q, seg_k = mla_weights(w_uq, w_ukv, q_norm, k_norm)
    q, k, vt = mla_prep(proj, positions, q_a_norm, kv_a_norm, wq, wkv, gq, gk, freq, seg_q, seg_k, batch, seq,
                        _tile(seq, 512, LANES), cq_blk, ckv_blk, kre_blk)
    n_heads = w_uq.shape[1]
    y_mla = mla_attention(q, k, vt, _tile(seq, 512, LANES), 4 if n_heads % 4 == 0 else 2).reshape(T, -1)

    w_router_pad = jnp.zeros((D, LANES), F32).at[:, :n_experts].set(w_router)
    w_router_hi = w_router_pad.astype(BF16)
    w_router_pad = jnp.stack([w_router_hi, (w_router_pad - w_router_hi.astype(F32)).astype(BF16)])
    router_bias_pad = jnp.zeros((1, LANES), F32).at[0, :n_experts].set(router_bias)
    x1, v2p, sel_e, sel_r, sel_w, cnt = merge_and_route(
        y_rnn, y_mla, proj, gr_blk, gm_blk, x2, mod, w_rnn_out.astype(BF16), w_mla_out.astype(BF16),
        w_out.astype(BF16), norm2, w_router_pad, router_bias_pad, n_experts, seq, _tile(seq, 256, 16))

    dest, fill, blk_e, n_used, n_rows = _routing_tables(cnt[0, :n_experts], sel_e, sel_r, T, n_experts)
    xs = moe_dispatch(v2p, dest, fill, n_rows, n_experts, _tile(seq, 512, 128))
    ys = moe_expert_ffn(xs, blk_e, n_used, w_gate.astype(BF16), w_up.astype(BF16), w_down.astype(BF16))
    return moe_shared_combine(v2p, x1, sel_w, dest, ys, mod, ws_gate.astype(BF16), ws_up.astype(BF16),
                              ws_down.astype(BF16), seq, _tile(seq, 256, 128))


def kernel(x, c, positions, w_mod, b_mod, norm1, w_in, conv_w, conv_b, w_a, b_a, w_i, b_i, lru_lambda,
           q_a_norm, kv_a_norm, w_uq, w_ukv, q_norm, k_norm, w_rnn_out, w_mla_out, w_out, norm2, w_router,
           router_bias, w_gate, w_up, w_down, ws_gate, ws_up, ws_down):
    B, S, D = x.shape
    x2 = x.reshape(B * S, D)
    per_layer = (w_mod, b_mod, norm1, w_in, conv_w, conv_b, w_a, b_a, w_i, b_i, lru_lambda, q_a_norm,
                 kv_a_norm, w_uq, w_ukv, q_norm, k_norm, w_rnn_out, w_mla_out, w_out, norm2, w_router,
                 router_bias, w_gate, w_up, w_down, ws_gate, ws_up, ws_down)
    for l in range(w_mod.shape[0]):
        x2 = _layer(x2, c, positions, B, S, *(w[l] for w in per_layer))
    return x2.reshape(B, S, D)
```

```python
import functools

import jax
import jax.numpy as jnp
from jax import lax
from jax.experimental import pallas as pl
from jax.experimental.pallas import tpu as pltpu

F32 = jnp.float32
BF16 = jnp.bfloat16

EPS = 1e-6
CHUNK = 64
LRU_C = 8.0
ROPE_THETA = 10000.0
TOP_K = 6
ROUTED_SCALE = 2.5
RNN_BLOCK_DIM = 128

LANES = 128
SUBLANES = 8
VMEM_LIMIT_BYTES = 56 * 1024 * 1024

MOE_BLOCK = 256
DEST_SLOTS = 8


def _cparams(semantics):
    return pltpu.CompilerParams(dimension_semantics=semantics, vmem_limit_bytes=VMEM_LIMIT_BYTES)


def _tile(n, target, align=LANES):
    if n <= target:
        return n
    t = (target // align) * align
    while t > align and n % t:
        t -= align
    assert n % t == 0, (n, target, align)
    return t


def _mod_kernel(c_ref, w_ref, b_ref, o_ref):
    c = c_ref[...]
    s = c * jax.nn.sigmoid(c)
    o_ref[...] = jnp.dot(s, w_ref[...], preferred_element_type=F32,
                         precision=lax.Precision.HIGHEST) + b_ref[...]


def adaln_mod(c, w_mod, b_mod):
    B, D = c.shape
    N = w_mod.shape[1]
    rows = -(-B // SUBLANES) * SUBLANES
    c_pad = jnp.zeros((rows, D), F32).at[:B].set(c)
    tn = _tile(N, 1024)
    out = pl.pallas_call(
        _mod_kernel,
        grid=(N // tn,),
        in_specs=[pl.BlockSpec((rows, D), lambda j: (0, 0)),
                  pl.BlockSpec((D, tn), lambda j: (0, j)),
                  pl.BlockSpec((1, tn), lambda j: (0, j))],
        out_specs=pl.BlockSpec((rows, tn), lambda j: (0, j)),
        out_shape=jax.ShapeDtypeStruct((rows, N), F32),
        compiler_params=_cparams(("arbitrary",)),
        name="adaln_mod",
    )(c_pad, w_mod, b_mod.reshape(1, N))
    return out[:B].reshape(B, N // D, D)


def _inproj_kernel(x_ref, mod_ref, g_ref, w_ref, o_ref):
    x = x_ref[...]
    ms = jnp.mean(x * x, axis=-1, keepdims=True)
    y = x * lax.rsqrt(ms + EPS) * g_ref[...]
    u = (y * (1.0 + mod_ref[0, 1:2, :]) + mod_ref[0, 0:1, :]).astype(BF16)
    o_ref[...] = jnp.dot(u, w_ref[...], preferred_element_type=F32).astype(o_ref.dtype)


def in_projection(x2, mod, norm1, w_cat, seq, tm):
    T, D = x2.shape
    N = w_cat.shape[1]
    per_b = seq // tm
    return pl.pallas_call(
        _inproj_kernel,
        grid=(T // tm,),
        in_specs=[pl.BlockSpec((tm, D), lambda i: (i, 0)),
                  pl.BlockSpec((1,) + mod.shape[1:], lambda i: (i // per_b, 0, 0)),
                  pl.BlockSpec((1, D), lambda i: (0, 0)),
                  pl.BlockSpec((D, N), lambda i: (0, 0), pipeline_mode=pl.Buffered(1))],
        out_specs=pl.BlockSpec((tm, N), lambda i: (i, 0)),
        out_shape=jax.ShapeDtypeStruct((T, N), BF16),
        compiler_params=_cparams(("arbitrary",)),
        name="in_projection",
    )(x2, mod, norm1.reshape(1, D), w_cat)


def _gelu_tanh(x):
    return 0.5 * x * (1.0 + jnp.tanh(0.7978845608028654 * (x + 0.044715 * (x * x * x))))


def _rglru_kernel(xr_ref, yg_ref, cw_ref, cb_ref, wai_ref, ba_ref, bi_ref, lam_ref, o_ref,
                  xbuf, xcbuf, abuf, bbuf, hc, *, conv_width):
    ts, dr = xcbuf.shape
    halo = SUBLANES
    nblk = dr // RNN_BLOCK_DIM

    @pl.when(pl.program_id(1) == 0)
    def _():
        xbuf[0:halo, :] = jnp.zeros((halo, dr), F32)
        hc[...] = jnp.zeros_like(hc)

    xbuf[halo:halo + ts, :] = xr_ref[...].astype(F32)
    xc = cb_ref[...] + xbuf[halo:halo + ts, :] * cw_ref[conv_width - 1:conv_width, :]
    for j in range(conv_width - 1):
        off = halo - (conv_width - 1) + j
        xc = xc + xbuf[off:off + ts, :] * cw_ref[j:j + 1, :]
    xcbuf[...] = xc
    xbuf[0:halo, :] = xbuf[ts:ts + halo, :]

    z = -lam_ref[...]
    sp = jnp.maximum(z, 0.0) + jnp.log1p(jnp.exp(-jnp.abs(z)))
    for k in range(nblk):
        cols = slice(k * RNN_BLOCK_DIM, (k + 1) * RNN_BLOCK_DIM)
        xk = xcbuf[:, cols]
        g = jnp.dot(xk.astype(BF16), wai_ref[k], preferred_element_type=F32)
        r = jax.nn.sigmoid(g[:, :RNN_BLOCK_DIM] + ba_ref[:, cols])
        ig = jax.nn.sigmoid(g[:, RNN_BLOCK_DIM:] + bi_ref[:, cols])
        a = jnp.exp((-LRU_C) * r * sp[:, cols])
        abuf[:, cols] = a
        bbuf[:, cols] = jnp.sqrt(1.0 - a * a) * (ig * xk)

    row = lax.broadcasted_iota(jnp.int32, (SUBLANES, dr), 0)
    rows2 = 2 * SUBLANES

    def scan8(a, b, h):
        for s in (1, 2, 4):
            m = row >= s
            a_s = jnp.where(m, pltpu.roll(a, s, 0), 1.0)
            b_s = jnp.where(m, pltpu.roll(b, s, 0), 0.0)
            b = a * b_s + b
            a = a * a_s
        hh = a * h + b
        return hh, jnp.broadcast_to(hh[SUBLANES - 1:SUBLANES, :], (SUBLANES, dr))

    def body(c, h):
        r0 = pl.multiple_of(c * rows2, rows2)
        a16 = abuf[pl.ds(r0, rows2), :]
        b16 = bbuf[pl.ds(r0, rows2), :]
        h0, h = scan8(a16[:SUBLANES], b16[:SUBLANES], h)
        h1, h = scan8(a16[SUBLANES:], b16[SUBLANES:], h)
        hs = jnp.concatenate([h0, h1], axis=0)
        yg = yg_ref[pl.ds(r0, rows2), :].astype(F32)
        o_ref[pl.ds(r0, rows2), :] = (hs * _gelu_tanh(yg)).astype(o_ref.dtype)
        return h

    hc[...] = lax.fori_loop(0, ts // rows2, body, hc[...])


def rglru_branch(proj, conv_w, conv_b, wai, b_a, b_i, lam, batch, seq, ts):
    T = proj.shape[0]
    W, dr = conv_w.shape
    nblk = dr // RNN_BLOCK_DIM
    per_b = seq // ts
    vec = lambda: pl.BlockSpec((1, dr), lambda b, i: (0, 0))
    return pl.pallas_call(
        functools.partial(_rglru_kernel, conv_width=W),
        grid=(batch, per_b),
        in_specs=[pl.BlockSpec((ts, dr), lambda b, i: (b * per_b + i, 0)),
                  pl.BlockSpec((ts, dr), lambda b, i: (b * per_b + i, 1)),
                  pl.BlockSpec((W, dr), lambda b, i: (0, 0)),
                  vec(),
                  pl.BlockSpec((nblk, RNN_BLOCK_DIM, 2 * RNN_BLOCK_DIM), lambda b, i: (0, 0, 0)),
                  vec(), vec(), vec()],
        out_specs=pl.BlockSpec((ts, dr), lambda b, i: (b * per_b + i, 0)),
        out_shape=jax.ShapeDtypeStruct((T, dr), BF16),
        scratch_shapes=[pltpu.VMEM((ts + SUBLANES, dr), F32),
                        pltpu.VMEM((ts, dr), F32),
                        pltpu.VMEM((ts, dr), F32),
                        pltpu.VMEM((ts, dr), F32),
                        pltpu.VMEM((SUBLANES, dr), F32)],
        compiler_params=_cparams(("arbitrary", "arbitrary")),
        name="rglru_branch",
    )(proj, proj, conv_w, conv_b.reshape(1, dr), wai, b_a.reshape(1, dr), b_i.reshape(1, dr),
      lam.reshape(1, dr))


QK_NOPE = 128
QK_ROPE = 64
V_HEAD = 128
QK_HEAD = QK_NOPE + QK_ROPE


def _rot_half(w):
    half = QK_ROPE // 2
    return jnp.concatenate([-w[..., half:], w[..., :half]], axis=-1)


def mla_weights(w_uq, w_ukv, q_norm, k_norm):
    r, H, _ = w_uq.shape
    half = QK_ROPE // 2
    pair = lambda w: w.reshape(w.shape[0], H // 2, 2 * w.shape[2])
    rope_w = w_uq[..., QK_NOPE:]
    wq = jnp.concatenate([pair(w_uq[..., :QK_NOPE]), pair(rope_w), pair(_rot_half(rope_w))], axis=-1)
    wkv = jnp.concatenate([pair(w_ukv[..., :QK_NOPE]), pair(w_ukv[..., QK_NOPE:])], axis=-1)
    wq = wq.transpose(1, 0, 2).astype(BF16)
    wkv = wkv.transpose(1, 0, 2).astype(BF16)

    def gains(g):
        gr = g[QK_NOPE:]
        swapped = jnp.concatenate([gr[half:], gr[:half]])
        return jnp.stack([g[:QK_NOPE], jnp.tile(gr, 2), jnp.tile(swapped, 2)]).astype(F32)

    inv_freq = ROPE_THETA ** (-jnp.arange(0, QK_ROPE, 2, dtype=F32) / QK_ROPE)
    freq = jnp.tile(inv_freq, 2 * LANES // QK_ROPE).reshape(1, LANES)

    rows = jnp.arange(3 * LANES)[:, None]
    cols = jnp.arange(2 * LANES)[None, :]
    head_b = cols >= LANES
    in_nope = (rows // LANES) == jnp.where(head_b, 1, 0)
    in_rope = (rows >= 2 * LANES) & (((rows - 2 * LANES) // QK_ROPE) == jnp.where(head_b, 1, 0))
    seg_q = (in_nope | in_rope).astype(BF16)
    seg_k = in_nope[:2 * LANES].astype(BF16)
    return wq, wkv, gains(q_norm), gains(k_norm), freq, seg_q, seg_k


def _mla_prep_kernel(cq_ref, ckv_ref, kre_ref, pos_ref, qan_ref, kvan_ref, wq_ref, wkv_ref,
                     gq_ref, gk_ref, freq_ref, segq_ref, segk_ref, q_ref, k_ref, vt_ref, cqn, ckvn):
    tm = cq_ref.shape[0]
    n_pairs = wq_ref.shape[0]

    def latent_norm(ref, g_ref):
        c = ref[...].astype(F32)
        ms = jnp.mean(c * c, axis=-1, keepdims=True)
        return (c * lax.rsqrt(ms + EPS) * g_ref[...]).astype(BF16)

    cqn[...] = latent_norm(cq_ref, qan_ref)
    ckvn[...] = latent_norm(ckv_ref, kvan_ref)

    lane = lax.broadcasted_iota(jnp.int32, (tm, LANES), 1)
    lo = lane < QK_ROPE
    ang = pos_ref[0].astype(F32) * freq_ref[...]
    cos_t, sin_t = jnp.cos(ang), jnp.sin(ang)
    q_cos = gq_ref[1:2, :] * cos_t
    q_sin = gq_ref[2:3, :] * sin_t
    q_scale = (QK_HEAD ** -0.5) * 1.4426950408889634

    kre = kre_ref[...].astype(F32)
    ss_kr = jnp.dot(jnp.where(lo, kre * kre, 0.0).astype(BF16), segk_ref[0:LANES, 0:LANES],
                    preferred_element_type=F32)
    z = kre * jnp.where(lo, gk_ref[1:2, :] * cos_t, gk_ref[2:3, :] * sin_t)
    k_rope = z + pltpu.roll(z, QK_ROPE, 1)
    k_rope_a = jnp.where(lo, k_rope, 0.0)
    k_rope_b = jnp.where(lo, 0.0, k_rope)

    def seg_sums(pieces, seg_ref):
        sq = jnp.concatenate([(p * p).astype(BF16) for p in pieces], axis=1)
        return jnp.dot(sq, seg_ref[...], preferred_element_type=F32)

    def pair(p, carry):
        qp = jnp.dot(cqn[...], wq_ref[p], preferred_element_type=F32)
        qa, qb, rope_raw, rot_raw = (qp[:, j * LANES:(j + 1) * LANES] for j in range(4))
        ss = seg_sums([qa, qb, rope_raw], segq_ref)
        inv = lax.rsqrt(ss * (1.0 / QK_HEAD) + EPS) * q_scale
        rope = rope_raw * q_cos + rot_raw * q_sin
        for j, qn in enumerate((qa, qb)):
            inv_j = inv[:, j * LANES:(j + 1) * LANES]
            q_ref[0, 2 * p + j] = jnp.concatenate(
                [qn * gq_ref[0:1, :] * inv_j, rope * inv_j], axis=1).astype(q_ref.dtype)

        kvp = jnp.dot(ckvn[...], wkv_ref[p], preferred_element_type=F32)
        ka, kb, va, vb = (kvp[:, j * LANES:(j + 1) * LANES] for j in range(4))
        ssk = seg_sums([ka, kb], segk_ref)
        for j, (kn, kr_j, vj) in enumerate(((ka, k_rope_a, va), (kb, k_rope_b, vb))):
            inv_j = lax.rsqrt((ssk[:, j * LANES:(j + 1) * LANES] + ss_kr) * (1.0 / QK_HEAD) + EPS)
            k_ref[0, 2 * p + j] = jnp.concatenate(
                [kn * gk_ref[0:1, :] * inv_j, kr_j * inv_j], axis=1).astype(k_ref.dtype)
            vt_ref[0, 2 * p + j] = vj.T.astype(vt_ref.dtype)
        return carry

    lax.fori_loop(0, n_pairs, pair, 0)


def mla_prep(proj, positions, q_a_norm, kv_a_norm, wq, wkv, gq, gk, freq, seg_q, seg_k, batch, seq, tm,
             cq_blk, ckv_blk, kre_blk):
    n_pairs, q_lora, _ = wq.shape
    kv_lora = wkv.shape[1]
    H = 2 * n_pairs
    per_b = seq // tm
    dq = QK_NOPE + LANES
    row = lambda blk: (lambda b, i: (b * per_b + i, blk))
    const2 = lambda b, i: (0, 0)
    const3 = lambda b, i: (0, 0, 0)
    out_map = lambda b, i: (b, 0, i, 0)
    return pl.pallas_call(
        _mla_prep_kernel,
        grid=(batch, per_b),
        in_specs=[pl.BlockSpec((tm, q_lora), row(cq_blk)),
                  pl.BlockSpec((tm, kv_lora), row(ckv_blk)),
                  pl.BlockSpec((tm, LANES), row(kre_blk)),
                  pl.BlockSpec((1, tm, 1), lambda b, i: (b, i, 0)),
                  pl.BlockSpec((1, q_lora), const2),
                  pl.BlockSpec((1, kv_lora), const2),
                  pl.BlockSpec(wq.shape, const3),
                  pl.BlockSpec(wkv.shape, const3),
                  pl.BlockSpec(gq.shape, const2),
                  pl.BlockSpec(gk.shape, const2),
                  pl.BlockSpec((1, LANES), const2),
                  pl.BlockSpec(seg_q.shape, const2),
                  pl.BlockSpec(seg_k.shape, const2)],
        out_specs=[pl.BlockSpec((1, H, tm, dq), out_map),
                   pl.BlockSpec((1, H, tm, dq), out_map),
                   pl.BlockSpec((1, H, V_HEAD, tm), lambda b, i: (b, 0, 0, i))],
        out_shape=[jax.ShapeDtypeStruct((batch, H, seq, dq), BF16),
                   jax.ShapeDtypeStruct((batch, H, seq, dq), BF16),
                   jax.ShapeDtypeStruct((batch, H, V_HEAD, seq), BF16)],
        scratch_shapes=[pltpu.VMEM((tm, q_lora), BF16), pltpu.VMEM((tm, kv_lora), BF16)],
        compiler_params=_cparams(("arbitrary", "arbitrary")),
        name="mla_prep",
    )(proj, proj, proj, positions.reshape(batch, seq, 1), q_a_norm.reshape(1, q_lora),
      kv_a_norm.reshape(1, kv_lora), wq, wkv, gq, gk, freq, seg_q, seg_k)


def _attn_kernel(q_ref, k_ref, vt_ref, o_ref, qt_ref, s0_ref, s1_ref, m_ref, l_ref, acc_ref):
    nh, tq = q_ref.shape[1], q_ref.shape[2]
    dv = vt_ref.shape[2]
    i = pl.program_id(2)
    m_ref[...] = jnp.full(m_ref.shape, -jnp.inf, F32)
    l_ref[...] = jnp.zeros(l_ref.shape, F32)
    acc_ref[...] = jnp.zeros(acc_ref.shape, F32)
    for h in range(nh):
        qt_ref[h] = q_ref[0, h].astype(F32).T.astype(qt_ref.dtype)

    def scores(j, s_ref):
        start = pl.multiple_of(j * tq, tq)
        for h in range(nh):
            s_ref[h] = jnp.dot(k_ref[0, h, pl.ds(start, tq), :], qt_ref[h], preferred_element_type=F32)

    def consume(j, s_ref, masked):
        start = pl.multiple_of(j * tq, tq)
        for h in range(nh):
            s = s_ref[h]
            vt = vt_ref[0, h, :, pl.ds(start, tq)]
            if masked:
                kc = lax.broadcasted_iota(jnp.int32, s.shape, 0) // CHUNK
                qc = lax.broadcasted_iota(jnp.int32, s.shape, 1) // CHUNK
                s = jnp.where(kc <= qc, s, -jnp.inf)
            m_prev = m_ref[h]
            m_new = jnp.maximum(m_prev, jnp.max(s, axis=0, keepdims=True))
            alpha = jnp.exp2(m_prev - m_new)
            p = jnp.exp2(s - m_new)
            l_ref[h] = alpha * l_ref[h] + jnp.sum(p, axis=0, keepdims=True)
            acc_ref[h] = alpha * acc_ref[h] + jnp.dot(vt, p.astype(vt.dtype), preferred_element_type=F32)
            m_ref[h] = m_new

    scores(0, s0_ref)

    def two_blocks(jj, carry):
        scores(2 * jj + 1, s1_ref)
        consume(2 * jj, s0_ref, False)
        scores(2 * jj + 2, s0_ref)
        consume(2 * jj + 1, s1_ref, False)
        return carry

    lax.fori_loop(0, i // 2, two_blocks, 0)

    @pl.when(i % 2 == 0)
    def _():
        consume(i, s0_ref, True)

    @pl.when(i % 2 == 1)
    def _():
        scores(i, s1_ref)
        consume(i - 1, s0_ref, False)
        consume(i, s1_ref, True)

    for h in range(nh):
        o_ref[0, :, h * dv:(h + 1) * dv] = (acc_ref[h] / l_ref[h]).T.astype(o_ref.dtype)


def mla_attention(q, k, vt, tq, nh):
    B, H, S, dq = q.shape
    dv = vt.shape[2]
    assert tq % CHUNK == 0 and S % tq == 0 and H % nh == 0
    head_map = lambda b, g, i: (b, g, 0, 0)
    return pl.pallas_call(
        _attn_kernel,
        grid=(B, H // nh, S // tq),
        in_specs=[pl.BlockSpec((1, nh, tq, dq), lambda b, g, i: (b, g, i, 0)),
                  pl.BlockSpec((1, nh, S, dq), head_map, pipeline_mode=pl.Buffered(1)),
                  pl.BlockSpec((1, nh, dv, S), head_map, pipeline_mode=pl.Buffered(1))],
        out_specs=pl.BlockSpec((1, tq, nh * dv), lambda b, g, i: (b, i, g)),
        out_shape=jax.ShapeDtypeStruct((B, S, H * dv), BF16),
        scratch_shapes=[pltpu.VMEM((nh, dq, tq), BF16),
                        pltpu.VMEM((nh, tq, tq), F32), pltpu.VMEM((nh, tq, tq), F32),
                        pltpu.VMEM((nh, 1, tq), F32), pltpu.VMEM((nh, 1, tq), F32),
                        pltpu.VMEM((nh, dv, tq), F32)],
        compiler_params=_cparams(("arbitrary", "arbitrary", "arbitrary")),
        name="mla_attention",
    )(q, k, vt)


def _merge_route_kernel(yr_ref, ym_ref, gr_ref, gm_ref, x_ref, mod_ref, wr_ref, wm_ref, wo_ref,
                        n2_ref, wrt_ref, rb_ref,
                        x1_ref, v2_ref, sel_e_ref, sel_r_ref, sel_w_ref, cnt_ref, cnt_acc,
                        *, n_experts):
    tm = x_ref.shape[0]

    @pl.when(pl.program_id(0) == 0)
    def _():
        cnt_acc[...] = jnp.zeros_like(cnt_acc)

    a = jnp.dot(yr_ref[...], wr_ref[...], preferred_element_type=F32)
    b = jnp.dot(ym_ref[...], wm_ref[...], preferred_element_type=F32)
    merged = (jax.nn.sigmoid(gr_ref[...].astype(F32)) * a
              + jax.nn.sigmoid(gm_ref[...].astype(F32)) * b)
    c = jnp.dot(merged.astype(BF16), wo_ref[...], preferred_element_type=F32)
    x1 = x_ref[...] + mod_ref[0, 2:3, :] * c
    x1_ref[...] = x1

    ms = jnp.mean(x1 * x1, axis=-1, keepdims=True)
    v = x1 * lax.rsqrt(ms + EPS) * n2_ref[...]
    v = v * (1.0 + mod_ref[0, 4:5, :]) + mod_ref[0, 3:4, :]
    _store_tokens(v2_ref, v)

    v_hi = v.astype(BF16)
    v_lo = (v - v_hi.astype(F32)).astype(BF16)
    logits = (jnp.dot(v_hi, wrt_ref[0], preferred_element_type=F32)
              + jnp.dot(v_hi, wrt_ref[1], preferred_element_type=F32)
              + jnp.dot(v_lo, wrt_ref[0], preferred_element_type=F32))
    scores = jax.nn.sigmoid(logits)
    lane = lax.broadcasted_iota(jnp.int32, (tm, LANES), 1)
    biased = jnp.where(lane < n_experts, scores + rb_ref[...], -jnp.inf)
    picked = jnp.zeros((tm, LANES), jnp.bool_)
    idxs, tops = [], []
    for _ in range(TOP_K):
        mx = jnp.max(biased, axis=-1, keepdims=True)
        idx = jnp.min(jnp.where(biased == mx, lane, LANES), axis=-1, keepdims=True)
        hit = lane == idx
        idxs.append(idx)
        tops.append(jnp.sum(jnp.where(hit, scores, 0.0), axis=-1, keepdims=True))
        biased = jnp.where(hit, -jnp.inf, biased)
        picked = jnp.logical_or(picked, hit)
    denom = tops[0]
    for t in tops[1:]:
        denom = denom + t

    onehot = jnp.where(picked, 1.0, 0.0)
    r_i = lax.broadcasted_iota(jnp.int32, (tm, tm), 0)
    c_i = lax.broadcasted_iota(jnp.int32, (tm, tm), 1)
    tri = jnp.where(c_i < r_i, 1.0, 0.0).astype(BF16)
    rank = jnp.dot(tri, onehot.astype(BF16), preferred_element_type=F32) + cnt_acc[0:1, :]
    cnt_acc[0:1, :] = cnt_acc[0:1, :] + jnp.sum(onehot, axis=0, keepdims=True)
    cnt_ref[...] = cnt_acc[...]

    sel_e = jnp.zeros((tm, LANES), jnp.int32)
    sel_r = jnp.zeros((tm, LANES), jnp.int32)
    sel_w = jnp.zeros((tm, LANES), F32)
    for kk in range(TOP_K):
        here = lane == kk
        rk = jnp.sum(jnp.where(lane == idxs[kk], rank, 0.0), axis=-1, keepdims=True)
        sel_e = jnp.where(here, idxs[kk], sel_e)
        sel_r = jnp.where(here, rk.astype(jnp.int32), sel_r)
        sel_w = jnp.where(here, tops[kk] / denom * ROUTED_SCALE, sel_w)
    sel_e_ref[...] = sel_e
    sel_r_ref[...] = sel_r
    sel_w_ref[...] = sel_w


def merge_and_route(y_rnn, y_mla, proj, gr_blk, gm_blk, x2, mod, w_rnn_out, w_mla_out, w_out, norm2,
                    w_router_pad, router_bias_pad, n_experts, seq, tm):
    T, D = x2.shape
    per_b = seq // tm
    token_rows = D // (2 * LANES)
    row = lambda blk: (lambda i: (i, blk))
    const = lambda i: (0, 0)
    resident = lambda shape: pl.BlockSpec(shape, const, pipeline_mode=pl.Buffered(1))
    lanes_out = lambda: pl.BlockSpec((tm, LANES), row(0))
    return pl.pallas_call(
        functools.partial(_merge_route_kernel, n_experts=n_experts),
        grid=(T // tm,),
        in_specs=[pl.BlockSpec((tm, D), row(0)),
                  pl.BlockSpec((tm, D), row(0)),
                  pl.BlockSpec((tm, D), row(gr_blk)),
                  pl.BlockSpec((tm, D), row(gm_blk)),
                  pl.BlockSpec((tm, D), row(0)),
                  pl.BlockSpec((1,) + mod.shape[1:], lambda i: (i // per_b, 0, 0)),
                  resident((D, D)), resident((D, D)), resident((D, D)),
                  pl.BlockSpec((1, D), const),
                  pl.BlockSpec((2, D, LANES), lambda i: (0, 0, 0)),
                  pl.BlockSpec((1, LANES), const)],
        out_specs=[pl.BlockSpec((tm, D), row(0)),
                   pl.BlockSpec((tm * token_rows, LANES), row(0)),
                   lanes_out(), lanes_out(), lanes_out(),
                   pl.BlockSpec((SUBLANES, LANES), const)],
        out_shape=[jax.ShapeDtypeStruct((T, D), F32),
                   jax.ShapeDtypeStruct((T * token_rows, LANES), jnp.uint32),
                   jax.ShapeDtypeStruct((T, LANES), jnp.int32),
                   jax.ShapeDtypeStruct((T, LANES), jnp.int32),
                   jax.ShapeDtypeStruct((T, LANES), F32),
                   jax.ShapeDtypeStruct((SUBLANES, LANES), F32)],
        scratch_shapes=[pltpu.VMEM((SUBLANES, LANES), F32)],
        compiler_params=_cparams(("arbitrary",)),
        name="merge_and_route",
    )(y_rnn, y_mla, proj, proj, x2, mod, w_rnn_out, w_mla_out, w_out, norm2.reshape(1, D),
      w_router_pad, router_bias_pad)


def _pack_pair(lo, hi):
    return pltpu.pack_elementwise([lo, hi], packed_dtype=BF16)


def _unpack_pair(p):
    lo = pltpu.unpack_elementwise(p, index=0, packed_dtype=BF16, unpacked_dtype=F32)
    hi = pltpu.unpack_elementwise(p, index=1, packed_dtype=BF16, unpacked_dtype=F32)
    return lo, hi


def _store_tokens(ref, x):
    n, d = x.shape
    half = d // 2
    rows = half // LANES
    for c in range(rows):
        lo = x[:, c * LANES:(c + 1) * LANES]
        hi = x[:, half + c * LANES:half + (c + 1) * LANES]
        ref[pl.ds(c, n, stride=rows), :] = _pack_pair(lo, hi)


def _load_tokens(ref, n):
    rows = ref.shape[0] // n
    los, his = [], []
    for c in range(rows):
        lo, hi = _unpack_pair(ref[pl.ds(c, n, stride=rows), :])
        los.append(lo)
        his.append(hi)
    return jnp.concatenate(los, axis=1), jnp.concatenate(his, axis=1)


def _load_tokens_bf16(ref, n):
    lo, hi = _load_tokens(ref, n)
    return jnp.concatenate([lo.astype(BF16), hi.astype(BF16)], axis=1)


def _swiglu(x, wg, wu, wd):
    g = jnp.dot(x, wg, preferred_element_type=F32)
    u = jnp.dot(x, wu, preferred_element_type=F32)
    h = (g * jax.nn.sigmoid(g)) * u
    return jnp.dot(h.astype(BF16), wd, preferred_element_type=F32)


def _token_window(ref, tok, rows, count=1):
    return ref.at[pl.ds(pl.multiple_of(tok * rows, rows), count * rows), :]


def _token_copy(src, src_tok, dst, dst_tok, sem, rows):
    return pltpu.make_async_copy(_token_window(src, src_tok, rows), _token_window(dst, dst_tok, rows), sem)


def _dispatch_kernel(dest_ref, fill_ref, v_ref, xs_hbm, zeros, sem, zsem, *, n_experts, rows):
    tm = v_ref.shape[0] // rows
    n_copies = tm * TOP_K

    def start(r, c):
        for kk in range(TOP_K):
            _token_copy(v_ref, r, xs_hbm, dest_ref[r * DEST_SLOTS + kk], sem, rows).start(priority=kk % 2)
        return c

    lax.fori_loop(0, tm, start, 0, unroll=2)

    @pl.when(pl.program_id(0) == 0)
    def _():
        zeros[...] = jnp.zeros_like(zeros)

        def per_expert(e, c):
            first, count = fill_ref[e], fill_ref[n_experts + e]

            def one(r, c2):
                _token_copy(zeros, 0, xs_hbm, first + r, zsem, rows).start()
                return c2

            def one_wait(r, c2):
                _token_copy(zeros, 0, xs_hbm, first, zsem, rows).wait()
                return c2

            lax.fori_loop(0, count, one, 0)
            lax.fori_loop(0, count, one_wait, 0)
            return c

        lax.fori_loop(0, n_experts, per_expert, 0)

        def block_copy(b):
            return pltpu.make_async_copy(zeros, _token_window(xs_hbm, b * MOE_BLOCK, rows, MOE_BLOCK), zsem)

        def tail(b, c):
            block_copy(b).start()
            block_copy(b).wait()
            return c

        lax.fori_loop(fill_ref[2 * n_experts], xs_hbm.shape[0] // (MOE_BLOCK * rows), tail, 0)

    done = _token_window(xs_hbm, 0, rows, n_copies)
    pltpu.make_async_copy(done, done, sem).wait()


def moe_dispatch(v2t, dest_flat, fill, n_slots, n_experts, rows, tm):
    T = v2t.shape[0] // rows
    return pl.pallas_call(
        functools.partial(_dispatch_kernel, n_experts=n_experts, rows=rows),
        grid=(T // tm,),
        in_specs=[pl.BlockSpec((tm * DEST_SLOTS,), lambda i: (i,), memory_space=pltpu.SMEM),
                  pl.BlockSpec(memory_space=pltpu.SMEM),
                  pl.BlockSpec((tm * rows, LANES), lambda i: (i, 0))],
        out_specs=pl.BlockSpec(memory_space=pl.ANY),
        out_shape=jax.ShapeDtypeStruct((n_slots * rows, LANES), jnp.uint32),
        scratch_shapes=[pltpu.VMEM((MOE_BLOCK * rows, LANES), jnp.uint32),
                        pltpu.SemaphoreType.DMA(()), pltpu.SemaphoreType.DMA(())],
        compiler_params=_cparams(("arbitrary",)),
        name="moe_dispatch",
    )(dest_flat, fill, v2t)


WEIGHT_SLAB = 128
WEIGHT_STAGES = 4


def _expert_ffn_kernel(sched_ref, nused_ref, xs_ref, wg_hbm, wu_hbm, wd_hbm, ys_ref,
                       wg_buf, wu_buf, wd_buf, stage, sems):
    i = pl.program_id(0)
    d_model, de = wg_buf.shape[1], wg_buf.shape[2]
    n_gu = d_model // WEIGHT_SLAB
    n_slabs = 2 * n_gu + de // WEIGHT_SLAB
    nb = pl.num_programs(0)
    expert, slot, nxt = sched_ref[i], sched_ref[nb + i], sched_ref[2 * nb + i]
    lo, hi, used = sched_ref[3 * nb + i], sched_ref[4 * nb + i], sched_ref[5 * nb + i]

    def slab_copy(e, s, j):
        def desc(w_hbm, k, width):
            r0 = pl.multiple_of((s - k) * WEIGHT_SLAB, WEIGHT_SLAB)
            return pltpu.make_async_copy(w_hbm.at[e, pl.ds(r0, WEIGHT_SLAB), :],
                                         stage.at[j, :, pl.ds(0, width)], sems.at[j])
        return (desc(wg_hbm, 0, de), desc(wu_hbm, n_gu, de), desc(wd_hbm, 2 * n_gu, d_model))

    def by_kind(s, fns):
        pl.when(s < n_gu)(fns[0])
        pl.when(jnp.logical_and(s >= n_gu, s < 2 * n_gu))(fns[1])
        pl.when(s >= 2 * n_gu)(fns[2])

    def start_slabs(e, first, last):
        def body(s, c):
            by_kind(s, [d.start for d in slab_copy(e, s, s - first)])
            return c
        lax.fori_loop(first, last, body, 0)

    def finish_slabs(e, first, last, dst_slot):
        def body(s, c):
            j = s - first
            descs = slab_copy(e, s, j)

            def land(d, buf, k, width):
                def f():
                    d.wait()
                    r0 = pl.multiple_of((s - k) * WEIGHT_SLAB, WEIGHT_SLAB)
                    buf[dst_slot, pl.ds(r0, WEIGHT_SLAB), :] = stage[j, :, 0:width].astype(BF16)
                return f

            by_kind(s, [land(descs[0], wg_buf, 0, de), land(descs[1], wu_buf, n_gu, de),
                        land(descs[2], wd_buf, 2 * n_gu, d_model)])
            return c
        lax.fori_loop(first, last, body, 0)

    def fetch_rounds(e, first, last, dst_slot):
        def body(r, c):
            a = first + r * WEIGHT_STAGES
            b = jnp.minimum(a + WEIGHT_STAGES, last)
            start_slabs(e, a, b)
            finish_slabs(e, a, b, dst_slot)
            return c
        lax.fori_loop(0, (last - first + WEIGHT_STAGES - 1) // WEIGHT_STAGES, body, 0)

    @pl.when(i == 0)
    def _():
        fetch_rounds(expert, 0, n_slabs, slot)

    mid = jnp.minimum(lo + WEIGHT_STAGES, hi)
    start_slabs(nxt, lo, mid)

    @pl.when(used == 1)
    def _():
        x = _load_tokens_bf16(xs_ref, MOE_BLOCK)
        _store_tokens(ys_ref, _swiglu(x, wg_buf[slot], wu_buf[slot], wd_buf[slot]))

    @pl.when(used == 0)
    def _():
        ys_ref[...] = jnp.zeros_like(ys_ref)

    finish_slabs(nxt, lo, mid, 1 - slot)
    fetch_rounds(nxt, mid, hi, 1 - slot)


def _expert_schedule(blk_e, n_used, counts, n_slabs):
    nb = blk_e.shape[0]
    n_experts = counts.shape[0]
    e_ids = jnp.arange(n_experts, dtype=jnp.int32)
    has = counts > 0
    blocks = (counts + MOE_BLOCK - 1) // MOE_BLOCK
    first_blk = jnp.cumsum(blocks) - blocks
    order = jnp.cumsum(has.astype(jnp.int32)) - 1
    later = jnp.where(has, e_ids, n_experts)
    nxt = jnp.concatenate([lax.cummin(later[::-1])[::-1][1:], jnp.full((1,), n_experts, jnp.int32)])
    blk = jnp.arange(nb, dtype=jnp.int32)
    used = blk < n_used
    pick = lambda table: jnp.sum(jnp.where(blk_e[:, None] == e_ids[None, :], table[None, :], 0), axis=1)
    nb_e, j = pick(blocks), blk - pick(first_blk)
    nxt_b = pick(nxt)
    per = (n_slabs + jnp.maximum(nb_e, 1) - 1) // jnp.maximum(nb_e, 1)
    fetch = used & (nxt_b < n_experts)
    lo = jnp.where(fetch, jnp.minimum(j * per, n_slabs), 0)
    hi = jnp.where(fetch, jnp.minimum((j + 1) * per, n_slabs), 0)
    return jnp.stack([blk_e, pick(order) % 2, jnp.minimum(nxt_b, n_experts - 1), lo, hi,
                      used.astype(jnp.int32)]).astype(jnp.int32)


def moe_expert_ffn(xs, blk_e, n_used, counts, w_gate, w_up, w_down, rows):
    blk_rows = MOE_BLOCK * rows
    nb = xs.shape[0] // blk_rows
    E, D, de = w_gate.shape
    assert D % WEIGHT_SLAB == 0 and de % WEIGHT_SLAB == 0
    sched = _expert_schedule(blk_e, n_used[0], counts, 2 * (D // WEIGHT_SLAB) + de // WEIGHT_SLAB)
    last = lambda i, s, nu: (jnp.minimum(i, nu[0] - 1), 0)
    hbm = pl.BlockSpec(memory_space=pl.ANY)
    return pl.pallas_call(
        _expert_ffn_kernel,
        grid_spec=pltpu.PrefetchScalarGridSpec(
            num_scalar_prefetch=2,
            grid=(nb,),
            in_specs=[pl.BlockSpec((blk_rows, LANES), last), hbm, hbm, hbm],
            out_specs=pl.BlockSpec((blk_rows, LANES), lambda i, s, nu: (i, 0)),
            scratch_shapes=[pltpu.VMEM((2, D, de), BF16), pltpu.VMEM((2, D, de), BF16),
                            pltpu.VMEM((2, de, D), BF16),
                            pltpu.VMEM((WEIGHT_STAGES, WEIGHT_SLAB, max(D, de)), F32),
                            pltpu.SemaphoreType.DMA((WEIGHT_STAGES,))]),
        out_shape=jax.ShapeDtypeStruct(xs.shape, jnp.uint32),
        compiler_params=_cparams(("arbitrary",)),
        name="moe_expert_ffn",
    )(sched.reshape(-1), n_used, xs, w_gate, w_up, w_down)


def _shared_combine_kernel(dest_ref, v_ref, x1_ref, w_ref, mod_ref, wg_ref, wu_ref, wd_ref, ys_hbm,
                           o_ref, ybuf, sem, *, rows):
    tm = x1_ref.shape[0]
    dh = x1_ref.shape[1] // 2
    n_copies = tm * TOP_K

    def start(r, c):
        for kk in range(TOP_K):
            _token_copy(ys_hbm, dest_ref[r * DEST_SLOTS + kk], ybuf.at[kk], r, sem, rows).start(priority=kk % 2)
        return c

    lax.fori_loop(0, tm, start, 0, unroll=2)
    shared = _swiglu(_load_tokens_bf16(v_ref, tm), wg_ref[...], wu_ref[...], wd_ref[...])
    done = _token_window(ys_hbm, 0, rows, n_copies)
    pltpu.make_async_copy(done, done, sem).wait()

    lo, hi = shared[:, :dh], shared[:, dh:]
    for kk in range(TOP_K):
        w = w_ref[:, kk:kk + 1]
        ylo, yhi = _load_tokens(ybuf.at[kk], tm)
        lo = lo + w * ylo
        hi = hi + w * yhi
    g2 = mod_ref[0, 5:6, :]
    o_ref[:, :dh] = x1_ref[:, :dh] + g2[:, :dh] * lo
    o_ref[:, dh:] = x1_ref[:, dh:] + g2[:, dh:] * hi


def moe_shared_combine(v2t, x1, sel_w, dest_flat, ys, mod, ws_gate, ws_up, ws_down, seq, rows, tm):
    T, D = x1.shape
    de = ws_gate.shape[1]
    per_b = seq // tm
    const = lambda i: (0, 0)
    resident = lambda shape: pl.BlockSpec(shape, const, pipeline_mode=pl.Buffered(1))
    return pl.pallas_call(
        functools.partial(_shared_combine_kernel, rows=rows),
        grid=(T // tm,),
        in_specs=[pl.BlockSpec((tm * DEST_SLOTS,), lambda i: (i,), memory_space=pltpu.SMEM),
                  pl.BlockSpec((tm * rows, LANES), lambda i: (i, 0)),
                  pl.BlockSpec((tm, D), lambda i: (i, 0)),
                  pl.BlockSpec((tm, LANES), lambda i: (i, 0)),
                  pl.BlockSpec((1,) + mod.shape[1:], lambda i: (i // per_b, 0, 0)),
                  resident((D, de)), resident((D, de)), resident((de, D)),
                  pl.BlockSpec(memory_space=pl.ANY)],
        out_specs=pl.BlockSpec((tm, D), lambda i: (i, 0)),
        out_shape=jax.ShapeDtypeStruct((T, D), F32),
        scratch_shapes=[pltpu.VMEM((TOP_K, tm * rows, LANES), jnp.uint32), pltpu.SemaphoreType.DMA(())],
        compiler_params=_cparams(("arbitrary",)),
        name="moe_shared_combine",
    )(dest_flat, v2t, x1, sel_w, mod, ws_gate, ws_up, ws_down, ys)


def _in_proj_layout(w_in, d_rnn, q_lora, kv_lora, d_model):
    o = [0]
    for wdt in (d_rnn, d_rnn, q_lora, kv_lora, QK_ROPE, d_model, d_model):
        o.append(o[-1] + wdt)
    xr, yg, cq, ckv, kr, g_rnn, g_mla = (w_in[:, o[i]:o[i + 1]] for i in range(7))
    pieces = [xr, yg, g_rnn, g_mla, cq, ckv, jnp.concatenate([kr, _rot_half(kr)], axis=1)]
    starts, pos = [], 0
    for p in pieces:
        assert pos % p.shape[1] == 0, "column block of the input projection is not aligned to its width"
        starts.append(pos // p.shape[1])
        pos += p.shape[1]
    n_pad = -(-pos // 512) * 512
    pieces.append(jnp.zeros((w_in.shape[0], n_pad - pos), w_in.dtype))
    return jnp.concatenate(pieces, axis=1).astype(BF16), starts


def _routing_tables(counts, sel_e, sel_r, n_tokens, n_experts):
    counts = counts.astype(jnp.int32)
    padded = (counts + MOE_BLOCK - 1) // MOE_BLOCK * MOE_BLOCK
    pad_end = jnp.cumsum(padded)
    pad_start = pad_end - padded
    nb = -(-(n_tokens * TOP_K) // MOE_BLOCK) + n_experts
    n_used = pad_end[-1] // MOE_BLOCK
    blk = jnp.arange(nb, dtype=jnp.int32)
    blk_e = jnp.sum((pad_end[None, :] <= (blk * MOE_BLOCK)[:, None]).astype(jnp.int32), axis=1)
    blk_e = jnp.minimum(blk_e, n_experts - 1)
    last_e = jnp.sum(jnp.where(blk == n_used - 1, blk_e, 0))
    blk_e = jnp.where(blk < n_used, blk_e, last_e)
    e_ids = jnp.arange(n_experts, dtype=jnp.int32)
    se, sr = sel_e[:, :DEST_SLOTS], sel_r[:, :DEST_SLOTS]
    dest = sr + jnp.sum(jnp.where(se[:, :, None] == e_ids, pad_start, 0), axis=-1)
    fill = jnp.concatenate([pad_start + counts, padded - counts, n_used.reshape(1)])
    return dest.reshape(-1), fill, blk_e, n_used.reshape(1), nb * MOE_BLOCK


def _layer(x2, c, positions, batch, seq, w_mod, b_mod, norm1, w_in, conv_w, conv_b, w_a, b_a, w_i, b_i,
           lru_lambda, q_a_norm, kv_a_norm, w_uq, w_ukv, q_norm, k_norm, w_rnn_out, w_mla_out, w_out,
           norm2, w_router, router_bias, w_gate, w_up, w_down, ws_gate, ws_up, ws_down):
    T, D = x2.shape
    d_rnn = conv_w.shape[1]
    q_lora, kv_lora = q_a_norm.shape[0], kv_a_norm.shape[0]
    n_experts = w_router.shape[1]
    assert w_uq.shape[2] == QK_HEAD and w_ukv.shape[2] == QK_NOPE + V_HEAD
    assert d_rnn == D and w_mla_out.shape[0] == D and n_experts <= LANES

    mod = adaln_mod(c, w_mod, b_mod)

    w_cat, (_, _, gr_blk, gm_blk, cq_blk, ckv_blk, kre_blk) = _in_proj_layout(w_in, d_rnn, q_lora, kv_lora, D)
    proj = in_projection(x2, mod, norm1, w_cat, seq, _tile(seq, 256, 16))

    wai = jnp.concatenate([w_a, w_i], axis=-1).astype(BF16)
    y_rnn = rglru_branch(proj, conv_w, conv_b, wai, b_a, b_i, lru_lambda, batch, seq, _tile(seq, 256, 16))

    wq, wkv, gq, gk, freq, seg_q, seg_k = mla_weights(w_uq, w_ukv, q_norm, k_norm)
    q, k, vt = mla_prep(proj, positions, q_a_norm, kv_a_norm, wq, wkv, gq, gk, freq, seg_q, seg_k, batch, seq,
                        _tile(seq, 512, LANES), cq_blk, ckv_blk, kre_blk)
    n_heads = w_uq.shape[1]
    y_mla = mla_attention(q, k, vt, _tile(seq, 512, LANES), 4 if n_heads % 4 == 0 else 2).reshape(T, -1)

    w_router_pad = jnp.zeros((D, LANES), F32).at[:, :n_experts].set(w_router)
    w_router_hi = w_router_pad.astype(BF16)
    w_router_pad = jnp.stack([w_router_hi, (w_router_pad - w_router_hi.astype(F32)).astype(BF16)])
    router_bias_pad = jnp.zeros((1, LANES), F32).at[0, :n_experts].set(router_bias)
    x1, v2t, sel_e, sel_r, sel_w, cnt = merge_and_route(
        y_rnn, y_mla, proj, gr_blk, gm_blk, x2, mod, w_rnn_out.astype(BF16), w_mla_out.astype(BF16),
        w_out.astype(BF16), norm2, w_router_pad, router_bias_pad, n_experts, seq, _tile(seq, 256, 16))

    token_rows = D // (2 * LANES)
    dest, fill, blk_e, n_used, n_slots = _routing_tables(cnt[0, :n_experts], sel_e, sel_r, T, n_experts)
    xs = moe_dispatch(v2t, dest, fill, n_slots, n_experts, token_rows, _tile(seq, 512, 128))
    ys = moe_expert_ffn(xs, blk_e, n_used, cnt[0, :n_experts].astype(jnp.int32), w_gate, w_up, w_down,
                        token_rows)
    return moe_shared_combine(v2t, x1, sel_w, dest, ys, mod, ws_gate.astype(BF16), ws_up.astype(BF16),
                              ws_down.astype(BF16), seq, token_rows, _tile(seq, 256, 128))


def kernel(x, c, positions, w_mod, b_mod, norm1, w_in, conv_w, conv_b, w_a, b_a, w_i, b_i, lru_lambda,
           q_a_norm, kv_a_norm, w_uq, w_ukv, q_norm, k_norm, w_rnn_out, w_mla_out, w_out, norm2, w_router,
           router_bias, w_gate, w_up, w_down, ws_gate, ws_up, ws_down):
    B, S, D = x.shape
    x2 = x.reshape(B * S, D)
    per_layer = (w_mod, b_mod, norm1, w_in, conv_w, conv_b, w_a, b_a, w_i, b_i, lru_lambda, q_a_norm,
                 kv_a_norm, w_uq, w_ukv, q_norm, k_norm, w_rnn_out, w_mla_out, w_out, norm2, w_router,
                 router_bias, w_gate, w_up, w_down, ws_gate, ws_up, ws_down)
    for l in range(w_mod.shape[0]):
        x2 = _layer(x2, c, positions, B, S, *(w[l] for w in per_layer))
    return x2.reshape(B, S, D)
```

```python
import functools

import jax
import jax.numpy as jnp
from jax import lax
from jax.experimental import pallas as pl
from jax.experimental.pallas import tpu as pltpu

F32 = jnp.float32
BF16 = jnp.bfloat16

EPS = 1e-6
CHUNK = 64
LRU_C = 8.0
ROPE_THETA = 10000.0
TOP_K = 6
ROUTED_SCALE = 2.5
RNN_BLOCK_DIM = 128

LANES = 128
SUBLANES = 8
VMEM_LIMIT_BYTES = 56 * 1024 * 1024

MOE_BLOCK = 256
DEST_SLOTS = 8


def _cparams(semantics):
    return pltpu.CompilerParams(dimension_semantics=semantics, vmem_limit_bytes=VMEM_LIMIT_BYTES)


def _tile(n, target, align=LANES):
    if n <= target:
        return n
    t = (target // align) * align
    while t > align and n % t:
        t -= align
    assert n % t == 0, (n, target, align)
    return t


def _mod_kernel(c_ref, w_ref, b_ref, o_ref):
    c = c_ref[...]
    s = c * jax.nn.sigmoid(c)
    o_ref[...] = jnp.dot(s, w_ref[...], preferred_element_type=F32,
                         precision=lax.Precision.HIGHEST) + b_ref[...]


def adaln_mod(c, w_mod, b_mod):
    B, D = c.shape
    N = w_mod.shape[1]
    rows = -(-B // SUBLANES) * SUBLANES
    c_pad = jnp.zeros((rows, D), F32).at[:B].set(c)
    tn = _tile(N, 1024)
    out = pl.pallas_call(
        _mod_kernel,
        grid=(N // tn,),
        in_specs=[pl.BlockSpec((rows, D), lambda j: (0, 0)),
                  pl.BlockSpec((D, tn), lambda j: (0, j)),
                  pl.BlockSpec((1, tn), lambda j: (0, j))],
        out_specs=pl.BlockSpec((rows, tn), lambda j: (0, j)),
        out_shape=jax.ShapeDtypeStruct((rows, N), F32),
        compiler_params=_cparams(("arbitrary",)),
        name="adaln_mod",
    )(c_pad, w_mod, b_mod.reshape(1, N))
    return out[:B].reshape(B, N // D, D)


def _inproj_kernel(x_ref, mod_ref, g_ref, w_ref, o_ref):
    x = x_ref[...]
    ms = jnp.mean(x * x, axis=-1, keepdims=True)
    y = x * lax.rsqrt(ms + EPS) * g_ref[...]
    u = (y * (1.0 + mod_ref[0, 1:2, :]) + mod_ref[0, 0:1, :]).astype(BF16)
    o_ref[...] = jnp.dot(u, w_ref[...], preferred_element_type=F32).astype(o_ref.dtype)


def in_projection(x2, mod, norm1, w_cat, seq, tm):
    T, D = x2.shape
    N = w_cat.shape[1]
    per_b = seq // tm
    return pl.pallas_call(
        _inproj_kernel,
        grid=(T // tm,),
        in_specs=[pl.BlockSpec((tm, D), lambda i: (i, 0)),
                  pl.BlockSpec((1,) + mod.shape[1:], lambda i: (i // per_b, 0, 0)),
                  pl.BlockSpec((1, D), lambda i: (0, 0)),
                  pl.BlockSpec((D, N), lambda i: (0, 0), pipeline_mode=pl.Buffered(1))],
        out_specs=pl.BlockSpec((tm, N), lambda i: (i, 0)),
        out_shape=jax.ShapeDtypeStruct((T, N), BF16),
        compiler_params=_cparams(("arbitrary",)),
        name="in_projection",
    )(x2, mod, norm1.reshape(1, D), w_cat)


def _gelu_tanh(x):
    return 0.5 * x * (1.0 + jnp.tanh(0.7978845608028654 * (x + 0.044715 * (x * x * x))))


def _rglru_kernel(xr_ref, yg_ref, cw_ref, cb_ref, wai_ref, ba_ref, bi_ref, lam_ref, o_ref,
                  xbuf, xcbuf, abuf, bbuf, hc, *, conv_width):
    ts, dr = xcbuf.shape
    halo = SUBLANES
    nblk = dr // RNN_BLOCK_DIM

    @pl.when(pl.program_id(1) == 0)
    def _():
        xbuf[0:halo, :] = jnp.zeros((halo, dr), F32)
        hc[...] = jnp.zeros_like(hc)

    xbuf[halo:halo + ts, :] = xr_ref[...].astype(F32)
    xc = cb_ref[...] + xbuf[halo:halo + ts, :] * cw_ref[conv_width - 1:conv_width, :]
    for j in range(conv_width - 1):
        off = halo - (conv_width - 1) + j
        xc = xc + xbuf[off:off + ts, :] * cw_ref[j:j + 1, :]
    xcbuf[...] = xc
    xbuf[0:halo, :] = xbuf[ts:ts + halo, :]

    z = -lam_ref[...]
    sp = jnp.maximum(z, 0.0) + jnp.log1p(jnp.exp(-jnp.abs(z)))
    for k in range(nblk):
        cols = slice(k * RNN_BLOCK_DIM, (k + 1) * RNN_BLOCK_DIM)
        xk = xcbuf[:, cols]
        g = jnp.dot(xk.astype(BF16), wai_ref[k], preferred_element_type=F32)
        r = jax.nn.sigmoid(g[:, :RNN_BLOCK_DIM] + ba_ref[:, cols])
        ig = jax.nn.sigmoid(g[:, RNN_BLOCK_DIM:] + bi_ref[:, cols])
        a = jnp.exp((-LRU_C) * r * sp[:, cols])
        abuf[:, cols] = a
        y = 1.0 - a * a
        bbuf[:, cols] = jnp.where(y > 0.0, y * lax.rsqrt(y), 0.0) * (ig * xk)

    row = lax.broadcasted_iota(jnp.int32, (SUBLANES, dr), 0)
    rows2 = 2 * SUBLANES

    def scan8(a, b, h):
        for s in (1, 2, 4):
            m = row >= s
            a_s = jnp.where(m, pltpu.roll(a, s, 0), 1.0)
            b_s = jnp.where(m, pltpu.roll(b, s, 0), 0.0)
            b = a * b_s + b
            a = a * a_s
        hh = a * h + b
        return hh, jnp.broadcast_to(hh[SUBLANES - 1:SUBLANES, :], (SUBLANES, dr))

    def body(c, h):
        r0 = pl.multiple_of(c * rows2, rows2)
        a16 = abuf[pl.ds(r0, rows2), :]
        b16 = bbuf[pl.ds(r0, rows2), :]
        h0, h = scan8(a16[:SUBLANES], b16[:SUBLANES], h)
        h1, h = scan8(a16[SUBLANES:], b16[SUBLANES:], h)
        hs = jnp.concatenate([h0, h1], axis=0)
        yg = yg_ref[pl.ds(r0, rows2), :].astype(F32)
        o_ref[pl.ds(r0, rows2), :] = (hs * _gelu_tanh(yg)).astype(o_ref.dtype)
        return h

    hc[...] = lax.fori_loop(0, ts // rows2, body, hc[...])


def rglru_branch(proj, conv_w, conv_b, wai, b_a, b_i, lam, batch, seq, ts):
    T = proj.shape[0]
    W, dr = conv_w.shape
    nblk = dr // RNN_BLOCK_DIM
    per_b = seq // ts
    vec = lambda: pl.BlockSpec((1, dr), lambda b, i: (0, 0))
    return pl.pallas_call(
        functools.partial(_rglru_kernel, conv_width=W),
        grid=(batch, per_b),
        in_specs=[pl.BlockSpec((ts, dr), lambda b, i: (b * per_b + i, 0)),
                  pl.BlockSpec((ts, dr), lambda b, i: (b * per_b + i, 1)),
                  pl.BlockSpec((W, dr), lambda b, i: (0, 0)),
                  vec(),
                  pl.BlockSpec((nblk, RNN_BLOCK_DIM, 2 * RNN_BLOCK_DIM), lambda b, i: (0, 0, 0)),
                  vec(), vec(), vec()],
        out_specs=pl.BlockSpec((ts, dr), lambda b, i: (b * per_b + i, 0)),
        out_shape=jax.ShapeDtypeStruct((T, dr), BF16),
        scratch_shapes=[pltpu.VMEM((ts + SUBLANES, dr), F32),
                        pltpu.VMEM((ts, dr), F32),
                        pltpu.VMEM((ts, dr), F32),
                        pltpu.VMEM((ts, dr), F32),
                        pltpu.VMEM((SUBLANES, dr), F32)],
        compiler_params=_cparams(("arbitrary", "arbitrary")),
        name="rglru_branch",
    )(proj, proj, conv_w, conv_b.reshape(1, dr), wai, b_a.reshape(1, dr), b_i.reshape(1, dr),
      lam.reshape(1, dr))


QK_NOPE = 128
QK_ROPE = 64
V_HEAD = 128
QK_HEAD = QK_NOPE + QK_ROPE


def _rot_half(w):
    half = QK_ROPE // 2
    return jnp.concatenate([-w[..., half:], w[..., :half]], axis=-1)


def mla_weights(w_uq, w_ukv, q_norm, k_norm):
    r, H, _ = w_uq.shape
    half = QK_ROPE // 2
    pair = lambda w: w.reshape(w.shape[0], H // 2, 2 * w.shape[2])
    rope_w = w_uq[..., QK_NOPE:]
    wq = jnp.concatenate([pair(w_uq[..., :QK_NOPE]), pair(rope_w), pair(_rot_half(rope_w))], axis=-1)
    wkv = jnp.concatenate([pair(w_ukv[..., :QK_NOPE]), pair(w_ukv[..., QK_NOPE:])], axis=-1)
    wq = wq.transpose(1, 0, 2).astype(BF16)
    wkv = wkv.transpose(1, 0, 2).astype(BF16)

    def gains(g):
        gr = g[QK_NOPE:]
        swapped = jnp.concatenate([gr[half:], gr[:half]])
        return jnp.stack([g[:QK_NOPE], jnp.tile(gr, 2), jnp.tile(swapped, 2)]).astype(F32)

    inv_freq = ROPE_THETA ** (-jnp.arange(0, QK_ROPE, 2, dtype=F32) / QK_ROPE)
    freq = jnp.tile(inv_freq, 2 * LANES // QK_ROPE).reshape(1, LANES)

    rows = jnp.arange(3 * LANES)[:, None]
    cols = jnp.arange(2 * LANES)[None, :]
    head_b = cols >= LANES
    in_nope = (rows // LANES) == jnp.where(head_b, 1, 0)
    in_rope = (rows >= 2 * LANES) & (((rows - 2 * LANES) // QK_ROPE) == jnp.where(head_b, 1, 0))
    seg_q = (in_nope | in_rope).astype(BF16)
    seg_k = in_nope[:2 * LANES].astype(BF16)
    return wq, wkv, gains(q_norm), gains(k_norm), freq, seg_q, seg_k


def _mla_prep_kernel(cq_ref, ckv_ref, kre_ref, pos_ref, qan_ref, kvan_ref, wq_ref, wkv_ref,
                     gq_ref, gk_ref, freq_ref, segq_ref, segk_ref, q_ref, k_ref, vt_ref, cqn, ckvn):
    tm = cq_ref.shape[0]
    n_pairs = wq_ref.shape[0]

    def latent_norm(ref, g_ref):
        c = ref[...].astype(F32)
        ms = jnp.mean(c * c, axis=-1, keepdims=True)
        return (c * lax.rsqrt(ms + EPS) * g_ref[...]).astype(BF16)

    cqn[...] = latent_norm(cq_ref, qan_ref)
    ckvn[...] = latent_norm(ckv_ref, kvan_ref)

    lane = lax.broadcasted_iota(jnp.int32, (tm, LANES), 1)
    lo = lane < QK_ROPE
    ang = pos_ref[0].astype(F32) * freq_ref[...]
    cos_t, sin_t = jnp.cos(ang), jnp.sin(ang)
    q_cos = gq_ref[1:2, :] * cos_t
    q_sin = gq_ref[2:3, :] * sin_t
    q_scale = (QK_HEAD ** -0.5) * 1.4426950408889634

    kre = kre_ref[...].astype(F32)
    ss_kr = jnp.dot(jnp.where(lo, kre * kre, 0.0).astype(BF16), segk_ref[0:LANES, 0:LANES],
                    preferred_element_type=F32)
    z = kre * jnp.where(lo, gk_ref[1:2, :] * cos_t, gk_ref[2:3, :] * sin_t)
    k_rope = z + pltpu.roll(z, QK_ROPE, 1)
    k_rope_a = jnp.where(lo, k_rope, 0.0)
    k_rope_b = jnp.where(lo, 0.0, k_rope)

    def seg_sums(pieces, seg_ref):
        sq = jnp.concatenate([(p * p).astype(BF16) for p in pieces], axis=1)
        return jnp.dot(sq, seg_ref[...], preferred_element_type=F32)

    def pair(p, carry):
        qp = jnp.dot(cqn[...], wq_ref[p], preferred_element_type=F32)
        qa, qb, rope_raw, rot_raw = (qp[:, j * LANES:(j + 1) * LANES] for j in range(4))
        ss = seg_sums([qa, qb, rope_raw], segq_ref)
        inv = lax.rsqrt(ss * (1.0 / QK_HEAD) + EPS) * q_scale
        rope = rope_raw * q_cos + rot_raw * q_sin
        for j, qn in enumerate((qa, qb)):
            inv_j = inv[:, j * LANES:(j + 1) * LANES]
            q_ref[0, 2 * p + j] = jnp.concatenate(
                [qn * gq_ref[0:1, :] * inv_j, rope * inv_j], axis=1).astype(q_ref.dtype)

        kvp = jnp.dot(ckvn[...], wkv_ref[p], preferred_element_type=F32)
        ka, kb, va, vb = (kvp[:, j * LANES:(j + 1) * LANES] for j in range(4))
        ssk = seg_sums([ka, kb], segk_ref)
        for j, (kn, kr_j, vj) in enumerate(((ka, k_rope_a, va), (kb, k_rope_b, vb))):
            inv_j = lax.rsqrt((ssk[:, j * LANES:(j + 1) * LANES] + ss_kr) * (1.0 / QK_HEAD) + EPS)
            k_ref[0, 2 * p + j] = jnp.concatenate(
                [kn * gk_ref[0:1, :] * inv_j, kr_j * inv_j], axis=1).astype(k_ref.dtype)
            vt_ref[0, 2 * p + j] = vj.T.astype(vt_ref.dtype)
        return carry

    lax.fori_loop(0, n_pairs, pair, 0)


def mla_prep(proj, positions, q_a_norm, kv_a_norm, wq, wkv, gq, gk, freq, seg_q, seg_k, batch, seq, tm,
             cq_blk, ckv_blk, kre_blk):
    n_pairs, q_lora, _ = wq.shape
    kv_lora = wkv.shape[1]
    H = 2 * n_pairs
    per_b = seq // tm
    dq = QK_NOPE + LANES
    row = lambda blk: (lambda b, i: (b * per_b + i, blk))
    const2 = lambda b, i: (0, 0)
    const3 = lambda b, i: (0, 0, 0)
    out_map = lambda b, i: (b, 0, i, 0)
    return pl.pallas_call(
        _mla_prep_kernel,
        grid=(batch, per_b),
        in_specs=[pl.BlockSpec((tm, q_lora), row(cq_blk)),
                  pl.BlockSpec((tm, kv_lora), row(ckv_blk)),
                  pl.BlockSpec((tm, LANES), row(kre_blk)),
                  pl.BlockSpec((1, tm, 1), lambda b, i: (b, i, 0)),
                  pl.BlockSpec((1, q_lora), const2),
                  pl.BlockSpec((1, kv_lora), const2),
                  pl.BlockSpec(wq.shape, const3),
                  pl.BlockSpec(wkv.shape, const3),
                  pl.BlockSpec(gq.shape, const2),
                  pl.BlockSpec(gk.shape, const2),
                  pl.BlockSpec((1, LANES), const2),
                  pl.BlockSpec(seg_q.shape, const2),
                  pl.BlockSpec(seg_k.shape, const2)],
        out_specs=[pl.BlockSpec((1, H, tm, dq), out_map),
                   pl.BlockSpec((1, H, tm, dq), out_map),
                   pl.BlockSpec((1, H, V_HEAD, tm), lambda b, i: (b, 0, 0, i))],
        out_shape=[jax.ShapeDtypeStruct((batch, H, seq, dq), BF16),
                   jax.ShapeDtypeStruct((batch, H, seq, dq), BF16),
                   jax.ShapeDtypeStruct((batch, H, V_HEAD, seq), BF16)],
        scratch_shapes=[pltpu.VMEM((tm, q_lora), BF16), pltpu.VMEM((tm, kv_lora), BF16)],
        compiler_params=_cparams(("arbitrary", "arbitrary")),
        name="mla_prep",
    )(proj, proj, proj, positions.reshape(batch, seq, 1), q_a_norm.reshape(1, q_lora),
      kv_a_norm.reshape(1, kv_lora), wq, wkv, gq, gk, freq, seg_q, seg_k)


def _attn_kernel(q_ref, k_ref, vt_ref, o_ref, qt_ref, s0_ref, s1_ref, m_ref, l_ref, acc_ref):
    nh, tq = q_ref.shape[1], q_ref.shape[2]
    dv = vt_ref.shape[2]
    i = pl.program_id(2)
    m_ref[...] = jnp.full(m_ref.shape, -jnp.inf, F32)
    l_ref[...] = jnp.zeros(l_ref.shape, F32)
    acc_ref[...] = jnp.zeros(acc_ref.shape, F32)
    for h in range(nh):
        qt_ref[h] = q_ref[0, h].astype(F32).T.astype(qt_ref.dtype)

    def scores(j, s_ref):
        start = pl.multiple_of(j * tq, tq)
        for h in range(nh):
            s_ref[h] = jnp.dot(k_ref[0, h, pl.ds(start, tq), :], qt_ref[h], preferred_element_type=F32)

    def consume(j, s_ref, masked):
        start = pl.multiple_of(j * tq, tq)
        for h in range(nh):
            s = s_ref[h]
            vt = vt_ref[0, h, :, pl.ds(start, tq)]
            if masked:
                kc = lax.broadcasted_iota(jnp.int32, s.shape, 0) // CHUNK
                qc = lax.broadcasted_iota(jnp.int32, s.shape, 1) // CHUNK
                s = jnp.where(kc <= qc, s, -jnp.inf)
            m_prev = m_ref[h]
            m_new = jnp.maximum(m_prev, jnp.max(s, axis=0, keepdims=True))
            alpha = jnp.exp2(m_prev - m_new)
            p = jnp.exp2(s - m_new)
            l_ref[h] = alpha * l_ref[h] + jnp.sum(p, axis=0, keepdims=True)
            acc_ref[h] = alpha * acc_ref[h] + jnp.dot(vt, p.astype(vt.dtype), preferred_element_type=F32)
            m_ref[h] = m_new

    scores(0, s0_ref)

    def two_blocks(jj, carry):
        scores(2 * jj + 1, s1_ref)
        consume(2 * jj, s0_ref, False)
        scores(2 * jj + 2, s0_ref)
        consume(2 * jj + 1, s1_ref, False)
        return carry

    lax.fori_loop(0, i // 2, two_blocks, 0)

    @pl.when(i % 2 == 0)
    def _():
        consume(i, s0_ref, True)

    @pl.when(i % 2 == 1)
    def _():
        scores(i, s1_ref)
        consume(i - 1, s0_ref, False)
        consume(i, s1_ref, True)

    for h in range(nh):
        o_ref[0, :, h * dv:(h + 1) * dv] = (acc_ref[h] / l_ref[h]).T.astype(o_ref.dtype)


def mla_attention(q, k, vt, tq, nh):
    B, H, S, dq = q.shape
    dv = vt.shape[2]
    assert tq % CHUNK == 0 and S % tq == 0 and H % nh == 0
    head_map = lambda b, g, i: (b, g, 0, 0)
    return pl.pallas_call(
        _attn_kernel,
        grid=(B, H // nh, S // tq),
        in_specs=[pl.BlockSpec((1, nh, tq, dq), lambda b, g, i: (b, g, i, 0)),
                  pl.BlockSpec((1, nh, S, dq), head_map, pipeline_mode=pl.Buffered(1)),
                  pl.BlockSpec((1, nh, dv, S), head_map, pipeline_mode=pl.Buffered(1))],
        out_specs=pl.BlockSpec((1, tq, nh * dv), lambda b, g, i: (b, i, g)),
        out_shape=jax.ShapeDtypeStruct((B, S, H * dv), BF16),
        scratch_shapes=[pltpu.VMEM((nh, dq, tq), BF16),
                        pltpu.VMEM((nh, tq, tq), F32), pltpu.VMEM((nh, tq, tq), F32),
                        pltpu.VMEM((nh, 1, tq), F32), pltpu.VMEM((nh, 1, tq), F32),
                        pltpu.VMEM((nh, dv, tq), F32)],
        compiler_params=_cparams(("arbitrary", "arbitrary", "arbitrary")),
        name="mla_attention",
    )(q, k, vt)


def _merge_route_kernel(yr_ref, ym_ref, gr_ref, gm_ref, x_ref, mod_ref, wr_ref, wm_ref, wo_ref,
                        n2_ref, wrt_ref, rb_ref,
                        x1_ref, v2_ref, sel_e_ref, sel_r_ref, sel_w_ref, cnt_ref, cnt_acc, logit_buf,
                        *, n_experts):
    tm = x_ref.shape[0]
    step = pl.program_id(0)

    @pl.when(step == 0)
    def _():
        cnt_acc[...] = jnp.zeros_like(cnt_acc)
        logit_buf[...] = jnp.zeros_like(logit_buf)

    a = jnp.dot(yr_ref[...], wr_ref[...], preferred_element_type=F32)
    b = jnp.dot(ym_ref[...], wm_ref[...], preferred_element_type=F32)
    merged = (jax.nn.sigmoid(gr_ref[...].astype(F32)) * a
              + jax.nn.sigmoid(gm_ref[...].astype(F32)) * b)
    c = jnp.dot(merged.astype(BF16), wo_ref[...], preferred_element_type=F32)
    x1 = x_ref[...] + mod_ref[0, 2:3, :] * c
    x1_ref[...] = x1

    ms = jnp.mean(x1 * x1, axis=-1, keepdims=True)
    v = x1 * lax.rsqrt(ms + EPS) * n2_ref[...]
    v = v * (1.0 + mod_ref[0, 4:5, :]) + mod_ref[0, 3:4, :]
    _store_tokens(v2_ref, v)

    v_hi = v.astype(BF16)
    v_lo = (v - v_hi.astype(F32)).astype(BF16)
    logits = (jnp.dot(v_hi, wrt_ref[0], preferred_element_type=F32)
              + jnp.dot(v_hi, wrt_ref[1], preferred_element_type=F32)
              + jnp.dot(v_lo, wrt_ref[0], preferred_element_type=F32))
    prev_logits = logit_buf[...]
    logit_buf[...] = logits

    scores = jax.nn.sigmoid(prev_logits)
    lane = lax.broadcasted_iota(jnp.int32, (tm, LANES), 1)
    biased = jnp.where(lane < n_experts, scores + rb_ref[...], -jnp.inf)
    picked = jnp.zeros((tm, LANES), jnp.bool_)
    idxs, tops = [], []
    for _ in range(TOP_K):
        mx = jnp.max(biased, axis=-1, keepdims=True)
        idx = jnp.min(jnp.where(biased == mx, lane, LANES), axis=-1, keepdims=True)
        hit = lane == idx
        idxs.append(idx)
        tops.append(jnp.sum(jnp.where(hit, scores, 0.0), axis=-1, keepdims=True))
        biased = jnp.where(hit, -jnp.inf, biased)
        picked = jnp.logical_or(picked, hit)
    denom = tops[0]
    for t in tops[1:]:
        denom = denom + t

    onehot = jnp.where(picked, 1.0, 0.0)
    r_i = lax.broadcasted_iota(jnp.int32, (tm, tm), 0)
    c_i = lax.broadcasted_iota(jnp.int32, (tm, tm), 1)
    tri = jnp.where(c_i < r_i, 1.0, 0.0).astype(BF16)
    rank = jnp.dot(tri, onehot.astype(BF16), preferred_element_type=F32) + cnt_acc[0:1, :]
    counted = jnp.where(step > 0, 1.0, 0.0)
    cnt_acc[0:1, :] = cnt_acc[0:1, :] + counted * jnp.sum(onehot, axis=0, keepdims=True)
    cnt_ref[...] = cnt_acc[...]

    sel_e = jnp.zeros((tm, LANES), jnp.int32)
    sel_r = jnp.zeros((tm, LANES), jnp.int32)
    sel_w = jnp.zeros((tm, LANES), F32)
    for kk in range(TOP_K):
        here = lane == kk
        rk = jnp.sum(jnp.where(lane == idxs[kk], rank, 0.0), axis=-1, keepdims=True)
        sel_e = jnp.where(here, idxs[kk], sel_e)
        sel_r = jnp.where(here, rk.astype(jnp.int32), sel_r)
        sel_w = jnp.where(here, tops[kk] / denom * ROUTED_SCALE, sel_w)
    sel_e_ref[...] = sel_e
    sel_r_ref[...] = sel_r
    sel_w_ref[...] = sel_w


def merge_and_route(y_rnn, y_mla, proj, gr_blk, gm_blk, x2, mod, w_rnn_out, w_mla_out, w_out, norm2,
                    w_router_pad, router_bias_pad, n_experts, seq, tm):
    T, D = x2.shape
    per_b = seq // tm
    token_rows = D // (2 * LANES)
    n_tiles = T // tm
    tile = lambda i: jnp.minimum(i, n_tiles - 1)
    row = lambda blk: (lambda i: (tile(i), blk))
    const = lambda i: (0, 0)
    resident = lambda shape: pl.BlockSpec(shape, const, pipeline_mode=pl.Buffered(1))
    lanes_out = lambda: pl.BlockSpec((tm, LANES), lambda i: (jnp.maximum(i - 1, 0), 0))
    return pl.pallas_call(
        functools.partial(_merge_route_kernel, n_experts=n_experts),
        grid=(n_tiles + 1,),
        in_specs=[pl.BlockSpec((tm, D), row(0)),
                  pl.BlockSpec((tm, D), row(0)),
                  pl.BlockSpec((tm, D), row(gr_blk)),
                  pl.BlockSpec((tm, D), row(gm_blk)),
                  pl.BlockSpec((tm, D), row(0)),
                  pl.BlockSpec((1,) + mod.shape[1:], lambda i: (tile(i) // per_b, 0, 0)),
                  resident((D, D)), resident((D, D)), resident((D, D)),
                  pl.BlockSpec((1, D), const),
                  pl.BlockSpec((2, D, LANES), lambda i: (0, 0, 0)),
                  pl.BlockSpec((1, LANES), const)],
        out_specs=[pl.BlockSpec((tm, D), row(0)),
                   pl.BlockSpec((tm * token_rows, LANES), row(0)),
                   lanes_out(), lanes_out(), lanes_out(),
                   pl.BlockSpec((SUBLANES, LANES), const)],
        out_shape=[jax.ShapeDtypeStruct((T, D), F32),
                   jax.ShapeDtypeStruct((T * token_rows, LANES), jnp.uint32),
                   jax.ShapeDtypeStruct((T, LANES), jnp.int32),
                   jax.ShapeDtypeStruct((T, LANES), jnp.int32),
                   jax.ShapeDtypeStruct((T, LANES), F32),
                   jax.ShapeDtypeStruct((SUBLANES, LANES), F32)],
        scratch_shapes=[pltpu.VMEM((SUBLANES, LANES), F32), pltpu.VMEM((tm, LANES), F32)],
        compiler_params=_cparams(("arbitrary",)),
        name="merge_and_route",
    )(y_rnn, y_mla, proj, proj, x2, mod, w_rnn_out, w_mla_out, w_out, norm2.reshape(1, D),
      w_router_pad, router_bias_pad)


def _pack_pair(lo, hi):
    return pltpu.pack_elementwise([lo, hi], packed_dtype=BF16)


def _unpack_pair(p):
    lo = pltpu.unpack_elementwise(p, index=0, packed_dtype=BF16, unpacked_dtype=F32)
    hi = pltpu.unpack_elementwise(p, index=1, packed_dtype=BF16, unpacked_dtype=F32)
    return lo, hi


def _store_tokens(ref, x):
    n, d = x.shape
    half = d // 2
    rows = half // LANES
    for c in range(rows):
        lo = x[:, c * LANES:(c + 1) * LANES]
        hi = x[:, half + c * LANES:half + (c + 1) * LANES]
        ref[pl.ds(c, n, stride=rows), :] = _pack_pair(lo, hi)


def _load_tokens(ref, n):
    rows = ref.shape[0] // n
    los, his = [], []
    for c in range(rows):
        lo, hi = _unpack_pair(ref[pl.ds(c, n, stride=rows), :])
        los.append(lo)
        his.append(hi)
    return jnp.concatenate(los, axis=1), jnp.concatenate(his, axis=1)


def _load_tokens_bf16(ref, n):
    lo, hi = _load_tokens(ref, n)
    return jnp.concatenate([lo.astype(BF16), hi.astype(BF16)], axis=1)


def _swiglu(x, wg, wu, wd):
    g = jnp.dot(x, wg, preferred_element_type=F32)
    u = jnp.dot(x, wu, preferred_element_type=F32)
    h = (g * jax.nn.sigmoid(g)) * u
    return jnp.dot(h.astype(BF16), wd, preferred_element_type=F32)


def _token_window(ref, tok, rows, count=1):
    return ref.at[pl.ds(pl.multiple_of(tok * rows, rows), count * rows), :]


def _token_copy(src, src_tok, dst, dst_tok, sem, rows):
    return pltpu.make_async_copy(_token_window(src, src_tok, rows), _token_window(dst, dst_tok, rows), sem)


def _dispatch_kernel(dest_ref, fill_ref, v_ref, xs_hbm, zeros, sem, zsem, *, n_experts, rows):
    tm = v_ref.shape[0] // rows
    n_copies = tm * TOP_K

    def start(r, c):
        for kk in range(TOP_K):
            _token_copy(v_ref, r, xs_hbm, dest_ref[r * DEST_SLOTS + kk], sem, rows).start(priority=kk % 2)
        return c

    lax.fori_loop(0, tm, start, 0, unroll=2)

    @pl.when(pl.program_id(0) == 0)
    def _():
        zeros[...] = jnp.zeros_like(zeros)

        def per_expert(e, c):
            first, count = fill_ref[e], fill_ref[n_experts + e]

            def one(r, c2):
                _token_copy(zeros, 0, xs_hbm, first + r, zsem, rows).start()
                return c2

            def one_wait(r, c2):
                _token_copy(zeros, 0, xs_hbm, first, zsem, rows).wait()
                return c2

            lax.fori_loop(0, count, one, 0)
            lax.fori_loop(0, count, one_wait, 0)
            return c

        lax.fori_loop(0, n_experts, per_expert, 0)

        def block_copy(b):
            return pltpu.make_async_copy(zeros, _token_window(xs_hbm, b * MOE_BLOCK, rows, MOE_BLOCK), zsem)

        def tail(b, c):
            block_copy(b).start()
            block_copy(b).wait()
            return c

        lax.fori_loop(fill_ref[2 * n_experts], xs_hbm.shape[0] // (MOE_BLOCK * rows), tail, 0)

    done = _token_window(xs_hbm, 0, rows, n_copies)
    pltpu.make_async_copy(done, done, sem).wait()


def moe_dispatch(v2t, dest_flat, fill, n_slots, n_experts, rows, tm):
    T = v2t.shape[0] // rows
    return pl.pallas_call(
        functools.partial(_dispatch_kernel, n_experts=n_experts, rows=rows),
        grid=(T // tm,),
        in_specs=[pl.BlockSpec((tm * DEST_SLOTS,), lambda i: (i,), memory_space=pltpu.SMEM),
                  pl.BlockSpec(memory_space=pltpu.SMEM),
                  pl.BlockSpec((tm * rows, LANES), lambda i: (i, 0))],
        out_specs=pl.BlockSpec(memory_space=pl.ANY),
        out_shape=jax.ShapeDtypeStruct((n_slots * rows, LANES), jnp.uint32),
        scratch_shapes=[pltpu.VMEM((MOE_BLOCK * rows, LANES), jnp.uint32),
                        pltpu.SemaphoreType.DMA(()), pltpu.SemaphoreType.DMA(())],
        compiler_params=_cparams(("arbitrary",)),
        name="moe_dispatch",
    )(dest_flat, fill, v2t)


WEIGHT_SLAB = 128
WEIGHT_STAGES = 4


def _expert_ffn_kernel(sched_ref, nused_ref, xs_ref, wg_hbm, wu_hbm, wd_hbm, ys_ref,
                       wg_buf, wu_buf, wd_buf, stage, sems):
    i = pl.program_id(0)
    d_model, de = wg_buf.shape[1], wg_buf.shape[2]
    n_gu = d_model // WEIGHT_SLAB
    n_slabs = 2 * n_gu + de // WEIGHT_SLAB
    nb = pl.num_programs(0)
    expert, slot, nxt = sched_ref[i], sched_ref[nb + i], sched_ref[2 * nb + i]
    lo, hi, used = sched_ref[3 * nb + i], sched_ref[4 * nb + i], sched_ref[5 * nb + i]

    def slab_copy(e, s, j):
        def desc(w_hbm, k, width):
            r0 = pl.multiple_of((s - k) * WEIGHT_SLAB, WEIGHT_SLAB)
            return pltpu.make_async_copy(w_hbm.at[e, pl.ds(r0, WEIGHT_SLAB), :],
                                         stage.at[j, :, pl.ds(0, width)], sems.at[j])
        return (desc(wg_hbm, 0, de), desc(wu_hbm, n_gu, de), desc(wd_hbm, 2 * n_gu, d_model))

    def by_kind(s, fns):
        pl.when(s < n_gu)(fns[0])
        pl.when(jnp.logical_and(s >= n_gu, s < 2 * n_gu))(fns[1])
        pl.when(s >= 2 * n_gu)(fns[2])

    def start_slabs(e, first, last):
        def body(s, c):
            by_kind(s, [d.start for d in slab_copy(e, s, s - first)])
            return c
        lax.fori_loop(first, last, body, 0)

    def finish_slabs(e, first, last, dst_slot):
        def body(s, c):
            j = s - first
            descs = slab_copy(e, s, j)

            def land(d, buf, k, width):
                def f():
                    d.wait()
                    r0 = pl.multiple_of((s - k) * WEIGHT_SLAB, WEIGHT_SLAB)
                    buf[dst_slot, pl.ds(r0, WEIGHT_SLAB), :] = stage[j, :, 0:width].astype(BF16)
                return f

            by_kind(s, [land(descs[0], wg_buf, 0, de), land(descs[1], wu_buf, n_gu, de),
                        land(descs[2], wd_buf, 2 * n_gu, d_model)])
            return c
        lax.fori_loop(first, last, body, 0)

    def fetch_rounds(e, first, last, dst_slot):
        def body(r, c):
            a = first + r * WEIGHT_STAGES
            b = jnp.minimum(a + WEIGHT_STAGES, last)
            start_slabs(e, a, b)
            finish_slabs(e, a, b, dst_slot)
            return c
        lax.fori_loop(0, (last - first + WEIGHT_STAGES - 1) // WEIGHT_STAGES, body, 0)

    @pl.when(i == 0)
    def _():
        fetch_rounds(expert, 0, n_slabs, slot)

    mid = jnp.minimum(lo + WEIGHT_STAGES, hi)
    start_slabs(nxt, lo, mid)

    @pl.when(used == 1)
    def _():
        x = _load_tokens_bf16(xs_ref, MOE_BLOCK)
        _store_tokens(ys_ref, _swiglu(x, wg_buf[slot], wu_buf[slot], wd_buf[slot]))

    @pl.when(used == 0)
    def _():
        ys_ref[...] = jnp.zeros_like(ys_ref)

    finish_slabs(nxt, lo, mid, 1 - slot)
    fetch_rounds(nxt, mid, hi, 1 - slot)


def _expert_schedule(blk_e, n_used, counts, n_slabs):
    nb = blk_e.shape[0]
    n_experts = counts.shape[0]
    e_ids = jnp.arange(n_experts, dtype=jnp.int32)
    has = counts > 0
    blocks = (counts + MOE_BLOCK - 1) // MOE_BLOCK
    first_blk = jnp.cumsum(blocks) - blocks
    order = jnp.cumsum(has.astype(jnp.int32)) - 1
    later = jnp.where(has, e_ids, n_experts)
    nxt = jnp.concatenate([lax.cummin(later[::-1])[::-1][1:], jnp.full((1,), n_experts, jnp.int32)])
    blk = jnp.arange(nb, dtype=jnp.int32)
    used = blk < n_used
    pick = lambda table: jnp.sum(jnp.where(blk_e[:, None] == e_ids[None, :], table[None, :], 0), axis=1)
    nb_e, j = pick(blocks), blk - pick(first_blk)
    nxt_b = pick(nxt)
    per = (n_slabs + jnp.maximum(nb_e, 1) - 1) // jnp.maximum(nb_e, 1)
    fetch = used & (nxt_b < n_experts)
    lo = jnp.where(fetch, jnp.minimum(j * per, n_slabs), 0)
    hi = jnp.where(fetch, jnp.minimum((j + 1) * per, n_slabs), 0)
    return jnp.stack([blk_e, pick(order) % 2, jnp.minimum(nxt_b, n_experts - 1), lo, hi,
                      used.astype(jnp.int32)]).astype(jnp.int32)


def moe_expert_ffn(xs, blk_e, n_used, counts, w_gate, w_up, w_down, rows):
    blk_rows = MOE_BLOCK * rows
    nb = xs.shape[0] // blk_rows
    E, D, de = w_gate.shape
    assert D % WEIGHT_SLAB == 0 and de % WEIGHT_SLAB == 0
    sched = _expert_schedule(blk_e, n_used[0], counts, 2 * (D // WEIGHT_SLAB) + de // WEIGHT_SLAB)
    last = lambda i, s, nu: (jnp.minimum(i, nu[0] - 1), 0)
    hbm = pl.BlockSpec(memory_space=pl.ANY)
    return pl.pallas_call(
        _expert_ffn_kernel,
        grid_spec=pltpu.PrefetchScalarGridSpec(
            num_scalar_prefetch=2,
            grid=(nb,),
            in_specs=[pl.BlockSpec((blk_rows, LANES), last), hbm, hbm, hbm],
            out_specs=pl.BlockSpec((blk_rows, LANES), lambda i, s, nu: (i, 0)),
            scratch_shapes=[pltpu.VMEM((2, D, de), BF16), pltpu.VMEM((2, D, de), BF16),
                            pltpu.VMEM((2, de, D), BF16),
                            pltpu.VMEM((WEIGHT_STAGES, WEIGHT_SLAB, max(D, de)), F32),
                            pltpu.SemaphoreType.DMA((WEIGHT_STAGES,))]),
        out_shape=jax.ShapeDtypeStruct(xs.shape, jnp.uint32),
        compiler_params=_cparams(("arbitrary",)),
        name="moe_expert_ffn",
    )(sched.reshape(-1), n_used, xs, w_gate, w_up, w_down)


def _shared_combine_kernel(dest_ref, dest_next_ref, v_ref, x1_ref, w_ref, mod_ref, wg_ref, wu_ref, wd_ref,
                           ys_hbm, o_ref, ybuf, sems, *, rows):
    tm = x1_ref.shape[0]
    dh = x1_ref.shape[1] // 2
    n_copies = tm * TOP_K
    step = pl.program_id(0)
    slot = step % 2

    def gather(d_ref, s):
        def start(r, c):
            for kk in range(TOP_K):
                _token_copy(ys_hbm, d_ref[r * DEST_SLOTS + kk], ybuf.at[s, kk], r, sems.at[s],
                            rows).start(priority=kk % 2)
            return c
        lax.fori_loop(0, tm, start, 0, unroll=2)

    @pl.when(step == 0)
    def _():
        gather(dest_ref, 0)

    @pl.when(step + 1 < pl.num_programs(0))
    def _():
        gather(dest_next_ref, 1 - slot)

    shared = _swiglu(_load_tokens_bf16(v_ref, tm), wg_ref[...], wu_ref[...], wd_ref[...])
    done = _token_window(ys_hbm, 0, rows, n_copies)
    pltpu.make_async_copy(done, done, sems.at[slot]).wait()

    lo, hi = shared[:, :dh], shared[:, dh:]
    for kk in range(TOP_K):
        w = w_ref[:, kk:kk + 1]
        ylo, yhi = _load_tokens(ybuf.at[slot, kk], tm)
        lo = lo + w * ylo
        hi = hi + w * yhi
    g2 = mod_ref[0, 5:6, :]
    o_ref[:, :dh] = x1_ref[:, :dh] + g2[:, :dh] * lo
    o_ref[:, dh:] = x1_ref[:, dh:] + g2[:, dh:] * hi


def moe_shared_combine(v2t, x1, sel_w, dest_flat, ys, mod, ws_gate, ws_up, ws_down, seq, rows, tm):
    T, D = x1.shape
    de = ws_gate.shape[1]
    per_b = seq // tm
    const = lambda i: (0, 0)
    resident = lambda shape: pl.BlockSpec(shape, const, pipeline_mode=pl.Buffered(1))
    return pl.pallas_call(
        functools.partial(_shared_combine_kernel, rows=rows),
        grid=(T // tm,),
        in_specs=[pl.BlockSpec((tm * DEST_SLOTS,), lambda i: (i,), memory_space=pltpu.SMEM),
                  pl.BlockSpec((tm * DEST_SLOTS,), lambda i: (jnp.minimum(i + 1, T // tm - 1),),
                               memory_space=pltpu.SMEM),
                  pl.BlockSpec((tm * rows, LANES), lambda i: (i, 0)),
                  pl.BlockSpec((tm, D), lambda i: (i, 0)),
                  pl.BlockSpec((tm, LANES), lambda i: (i, 0)),
                  pl.BlockSpec((1,) + mod.shape[1:], lambda i: (i // per_b, 0, 0)),
                  resident((D, de)), resident((D, de)), resident((de, D)),
                  pl.BlockSpec(memory_space=pl.ANY)],
        out_specs=pl.BlockSpec((tm, D), lambda i: (i, 0)),
        out_shape=jax.ShapeDtypeStruct((T, D), F32),
        scratch_shapes=[pltpu.VMEM((2, TOP_K, tm * rows, LANES), jnp.uint32), pltpu.SemaphoreType.DMA((2,))],
        compiler_params=_cparams(("arbitrary",)),
        name="moe_shared_combine",
    )(dest_flat, dest_flat, v2t, x1, sel_w, mod, ws_gate, ws_up, ws_down, ys)


def _in_proj_layout(w_in, d_rnn, q_lora, kv_lora, d_model):
    o = [0]
    for wdt in (d_rnn, d_rnn, q_lora, kv_lora, QK_ROPE, d_model, d_model):
        o.append(o[-1] + wdt)
    xr, yg, cq, ckv, kr, g_rnn, g_mla = (w_in[:, o[i]:o[i + 1]] for i in range(7))
    pieces = [xr, yg, g_rnn, g_mla, cq, ckv, jnp.concatenate([kr, _rot_half(kr)], axis=1)]
    starts, pos = [], 0
    for p in pieces:
        assert pos % p.shape[1] == 0, "column block of the input projection is not aligned to its width"
        starts.append(pos // p.shape[1])
        pos += p.shape[1]
    n_pad = -(-pos // 512) * 512
    pieces.append(jnp.zeros((w_in.shape[0], n_pad - pos), w_in.dtype))
    return jnp.concatenate(pieces, axis=1).astype(BF16), starts


def _routing_tables(counts, sel_e, sel_r, n_tokens, n_experts):
    counts = counts.astype(jnp.int32)
    padded = (counts + MOE_BLOCK - 1) // MOE_BLOCK * MOE_BLOCK
    pad_end = jnp.cumsum(padded)
    pad_start = pad_end - padded
    nb = -(-(n_tokens * TOP_K) // MOE_BLOCK) + n_experts
    n_used = pad_end[-1] // MOE_BLOCK
    blk = jnp.arange(nb, dtype=jnp.int32)
    blk_e = jnp.sum((pad_end[None, :] <= (blk * MOE_BLOCK)[:, None]).astype(jnp.int32), axis=1)
    blk_e = jnp.minimum(blk_e, n_experts - 1)
    last_e = jnp.sum(jnp.where(blk == n_used - 1, blk_e, 0))
    blk_e = jnp.where(blk < n_used, blk_e, last_e)
    e_ids = jnp.arange(n_experts, dtype=jnp.int32)
    se, sr = sel_e[:, :DEST_SLOTS], sel_r[:, :DEST_SLOTS]
    dest = sr + jnp.sum(jnp.where(se[:, :, None] == e_ids, pad_start, 0), axis=-1)
    fill = jnp.concatenate([pad_start + counts, padded - counts, n_used.reshape(1)])
    return dest.reshape(-1), fill, blk_e, n_used.reshape(1), nb * MOE_BLOCK


def _layer(x2, c, positions, batch, seq, w_mod, b_mod, norm1, w_in, conv_w, conv_b, w_a, b_a, w_i, b_i,
           lru_lambda, q_a_norm, kv_a_norm, w_uq, w_ukv, q_norm, k_norm, w_rnn_out, w_mla_out, w_out,
           norm2, w_router, router_bias, w_gate, w_up, w_down, ws_gate, ws_up, ws_down):
    T, D = x2.shape
    d_rnn = conv_w.shape[1]
    q_lora, kv_lora = q_a_norm.shape[0], kv_a_norm.shape[0]
    n_experts = w_router.shape[1]
    assert w_uq.shape[2] == QK_HEAD and w_ukv.shape[2] == QK_NOPE + V_HEAD
    assert d_rnn == D and w_mla_out.shape[0] == D and n_experts <= LANES

    mod = adaln_mod(c, w_mod, b_mod)

    w_cat, (_, _, gr_blk, gm_blk, cq_blk, ckv_blk, kre_blk) = _in_proj_layout(w_in, d_rnn, q_lora, kv_lora, D)
    proj = in_projection(x2, mod, norm1, w_cat, seq, _tile(seq, 256, 16))

    wai = jnp.concatenate([w_a, w_i], axis=-1).astype(BF16)
    y_rnn = rglru_branch(proj, conv_w, conv_b, wai, b_a, b_i, lru_lambda, batch, seq, _tile(seq, 256, 16))

    wq, wkv, gq, gk, freq, seg_q, seg_k = mla_weights(w_uq, w_ukv, q_norm, k_norm)
    q, k, vt = mla_prep(proj, positions, q_a_norm, kv_a_norm, wq, wkv, gq, gk, freq, seg_q, seg_k, batch, seq,
                        _tile(seq, 512, LANES), cq_blk, ckv_blk, kre_blk)
    n_heads = w_uq.shape[1]
    y_mla = mla_attention(q, k, vt, _tile(seq, 512, LANES), 4 if n_heads % 4 == 0 else 2).reshape(T, -1)

    w_router_pad = jnp.zeros((D, LANES), F32).at[:, :n_experts].set(w_router)
    w_router_hi = w_router_pad.astype(BF16)
    w_router_pad = jnp.stack([w_router_hi, (w_router_pad - w_router_hi.astype(F32)).astype(BF16)])
    router_bias_pad = jnp.zeros((1, LANES), F32).at[0, :n_experts].set(router_bias)
    x1, v2t, sel_e, sel_r, sel_w, cnt = merge_and_route(
        y_rnn, y_mla, proj, gr_blk, gm_blk, x2, mod, w_rnn_out.astype(BF16), w_mla_out.astype(BF16),
        w_out.astype(BF16), norm2, w_router_pad, router_bias_pad, n_experts, seq, _tile(seq, 256, 16))

    token_rows = D // (2 * LANES)
    dest, fill, blk_e, n_used, n_slots = _routing_tables(cnt[0, :n_experts], sel_e, sel_r, T, n_experts)
    xs = moe_dispatch(v2t, dest, fill, n_slots, n_experts, token_rows, _tile(seq, 512, 128))
    ys = moe_expert_ffn(xs, blk_e, n_used, cnt[0, :n_experts].astype(jnp.int32), w_gate, w_up, w_down,
                        token_rows)
    return moe_shared_combine(v2t, x1, sel_w, dest, ys, mod, ws_gate.astype(BF16), ws_up.astype(BF16),
                              ws_down.astype(BF16), seq, token_rows, _tile(seq, 256, 128))


def kernel(x, c, positions, w_mod, b_mod, norm1, w_in, conv_w, conv_b, w_a, b_a, w_i, b_i, lru_lambda,
           q_a_norm, kv_a_norm, w_uq, w_ukv, q_norm, k_norm, w_rnn_out, w_mla_out, w_out, norm2, w_router,
           router_bias, w_gate, w_up, w_down, ws_gate, ws_up, ws_down):
    B, S, D = x.shape
    x2 = x.reshape(B * S, D)
    per_layer = (w_mod, b_mod, norm1, w_in, conv_w, conv_b, w_a, b_a, w_i, b_i, lru_lambda, q_a_norm,
                 kv_a_norm, w_uq, w_ukv, q_norm, k_norm, w_rnn_out, w_mla_out, w_out, norm2, w_router,
                 router_bias, w_gate, w_up, w_down, ws_gate, ws_up, ws_down)
    for l in range(w_mod.shape[0]):
        x2 = _layer(x2, c, positions, B, S, *(w[l] for w in per_layer))
    return x2.reshape(B, S, D)
```

```python
import functools

import jax
import jax.numpy as jnp
from jax import lax
from jax.experimental import pallas as pl
from jax.experimental.pallas import tpu as pltpu

F32 = jnp.float32
BF16 = jnp.bfloat16

EPS = 1e-6
CHUNK = 64
LRU_C = 8.0
ROPE_THETA = 10000.0
TOP_K = 6
ROUTED_SCALE = 2.5
RNN_BLOCK_DIM = 128

LANES = 128
SUBLANES = 8
VMEM_LIMIT_BYTES = 56 * 1024 * 1024

MOE_BLOCK = 256
DEST_SLOTS = 8


def _cparams(semantics):
    return pltpu.CompilerParams(dimension_semantics=semantics, vmem_limit_bytes=VMEM_LIMIT_BYTES)


def _tile(n, target, align=LANES):
    if n <= target:
        return n
    t = (target // align) * align
    while t > align and n % t:
        t -= align
    assert n % t == 0, (n, target, align)
    return t


def _mod_kernel(c_ref, w_ref, b_ref, o_ref):
    c = c_ref[...]
    s = c * jax.nn.sigmoid(c)
    o_ref[...] = jnp.dot(s, w_ref[...], preferred_element_type=F32,
                         precision=lax.Precision.HIGHEST) + b_ref[...]


def adaln_mod(c, w_mod, b_mod):
    B, D = c.shape
    N = w_mod.shape[1]
    rows = -(-B // SUBLANES) * SUBLANES
    c_pad = jnp.zeros((rows, D), F32).at[:B].set(c)
    tn = _tile(N, 1024)
    out = pl.pallas_call(
        _mod_kernel,
        grid=(N // tn,),
        in_specs=[pl.BlockSpec((rows, D), lambda j: (0, 0)),
                  pl.BlockSpec((D, tn), lambda j: (0, j)),
                  pl.BlockSpec((1, tn), lambda j: (0, j))],
        out_specs=pl.BlockSpec((rows, tn), lambda j: (0, j)),
        out_shape=jax.ShapeDtypeStruct((rows, N), F32),
        compiler_params=_cparams(("arbitrary",)),
        name="adaln_mod",
    )(c_pad, w_mod, b_mod.reshape(1, N))
    return out[:B].reshape(B, N // D, D)


def _inproj_kernel(x_ref, mod_ref, g_ref, w_ref, o_ref):
    x = x_ref[...]
    ms = jnp.mean(x * x, axis=-1, keepdims=True)
    y = x * lax.rsqrt(ms + EPS) * g_ref[...]
    u = (y * (1.0 + mod_ref[0, 1:2, :]) + mod_ref[0, 0:1, :]).astype(BF16)
    o_ref[...] = jnp.dot(u, w_ref[...], preferred_element_type=F32).astype(o_ref.dtype)


def in_projection(x2, mod, norm1, w_cat, seq, tm):
    T, D = x2.shape
    N = w_cat.shape[1]
    per_b = seq // tm
    return pl.pallas_call(
        _inproj_kernel,
        grid=(T // tm,),
        in_specs=[pl.BlockSpec((tm, D), lambda i: (i, 0)),
                  pl.BlockSpec((1,) + mod.shape[1:], lambda i: (i // per_b, 0, 0)),
                  pl.BlockSpec((1, D), lambda i: (0, 0)),
                  pl.BlockSpec((D, N), lambda i: (0, 0), pipeline_mode=pl.Buffered(1))],
        out_specs=pl.BlockSpec((tm, N), lambda i: (i, 0)),
        out_shape=jax.ShapeDtypeStruct((T, N), BF16),
        compiler_params=_cparams(("arbitrary",)),
        name="in_projection",
    )(x2, mod, norm1.reshape(1, D), w_cat)


def _gelu_tanh(x):
    return 0.5 * x * (1.0 + jnp.tanh(0.7978845608028654 * (x + 0.044715 * (x * x * x))))


def _rglru_kernel(xr_ref, yg_ref, cw_ref, cb_ref, wai_ref, ba_ref, bi_ref, lam_ref, o_ref,
                  xbuf, xcbuf, abuf, bbuf, hc, *, conv_width):
    ts, dr = xcbuf.shape
    halo = SUBLANES
    nblk = dr // RNN_BLOCK_DIM

    @pl.when(pl.program_id(1) == 0)
    def _():
        xbuf[0:halo, :] = jnp.zeros((halo, dr), F32)
        hc[...] = jnp.zeros_like(hc)

    xbuf[halo:halo + ts, :] = xr_ref[...].astype(F32)
    xc = cb_ref[...] + xbuf[halo:halo + ts, :] * cw_ref[conv_width - 1:conv_width, :]
    for j in range(conv_width - 1):
        off = halo - (conv_width - 1) + j
        xc = xc + xbuf[off:off + ts, :] * cw_ref[j:j + 1, :]
    xcbuf[...] = xc
    xbuf[0:halo, :] = xbuf[ts:ts + halo, :]

    z = -lam_ref[...]
    sp = jnp.maximum(z, 0.0) + jnp.log1p(jnp.exp(-jnp.abs(z)))
    for k in range(nblk):
        cols = slice(k * RNN_BLOCK_DIM, (k + 1) * RNN_BLOCK_DIM)
        xk = xcbuf[:, cols]
        g = jnp.dot(xk.astype(BF16), wai_ref[k], preferred_element_type=F32)
        r = jax.nn.sigmoid(g[:, :RNN_BLOCK_DIM] + ba_ref[:, cols])
        ig = jax.nn.sigmoid(g[:, RNN_BLOCK_DIM:] + bi_ref[:, cols])
        a = jnp.exp((-LRU_C) * r * sp[:, cols])
        abuf[:, cols] = a
        y = 1.0 - a * a
        bbuf[:, cols] = jnp.where(y > 0.0, y * lax.rsqrt(y), 0.0) * (ig * xk)

    row = lax.broadcasted_iota(jnp.int32, (SUBLANES, dr), 0)
    rows2 = 2 * SUBLANES

    def scan8(a, b, h):
        for s in (1, 2, 4):
            m = row >= s
            a_s = jnp.where(m, pltpu.roll(a, s, 0), 1.0)
            b_s = jnp.where(m, pltpu.roll(b, s, 0), 0.0)
            b = a * b_s + b
            a = a * a_s
        hh = a * h + b
        return hh, jnp.broadcast_to(hh[SUBLANES - 1:SUBLANES, :], (SUBLANES, dr))

    def body(c, h):
        r0 = pl.multiple_of(c * rows2, rows2)
        a16 = abuf[pl.ds(r0, rows2), :]
        b16 = bbuf[pl.ds(r0, rows2), :]
        h0, h = scan8(a16[:SUBLANES], b16[:SUBLANES], h)
        h1, h = scan8(a16[SUBLANES:], b16[SUBLANES:], h)
        hs = jnp.concatenate([h0, h1], axis=0)
        yg = yg_ref[pl.ds(r0, rows2), :].astype(F32)
        o_ref[pl.ds(r0, rows2), :] = (hs * _gelu_tanh(yg)).astype(o_ref.dtype)
        return h

    hc[...] = lax.fori_loop(0, ts // rows2, body, hc[...])


def rglru_branch(proj, conv_w, conv_b, wai, b_a, b_i, lam, batch, seq, ts):
    T = proj.shape[0]
    W, dr = conv_w.shape
    nblk = dr // RNN_BLOCK_DIM
    per_b = seq // ts
    vec = lambda: pl.BlockSpec((1, dr), lambda b, i: (0, 0))
    return pl.pallas_call(
        functools.partial(_rglru_kernel, conv_width=W),
        grid=(batch, per_b),
        in_specs=[pl.BlockSpec((ts, dr), lambda b, i: (b * per_b + i, 0)),
                  pl.BlockSpec((ts, dr), lambda b, i: (b * per_b + i, 1)),
                  pl.BlockSpec((W, dr), lambda b, i: (0, 0)),
                  vec(),
                  pl.BlockSpec((nblk, RNN_BLOCK_DIM, 2 * RNN_BLOCK_DIM), lambda b, i: (0, 0, 0)),
                  vec(), vec(), vec()],
        out_specs=pl.BlockSpec((ts, dr), lambda b, i: (b * per_b + i, 0)),
        out_shape=jax.ShapeDtypeStruct((T, dr), BF16),
        scratch_shapes=[pltpu.VMEM((ts + SUBLANES, dr), F32),
                        pltpu.VMEM((ts, dr), F32),
                        pltpu.VMEM((ts, dr), F32),
                        pltpu.VMEM((ts, dr), F32),
                        pltpu.VMEM((SUBLANES, dr), F32)],
        compiler_params=_cparams(("arbitrary", "arbitrary")),
        name="rglru_branch",
    )(proj, proj, conv_w, conv_b.reshape(1, dr), wai, b_a.reshape(1, dr), b_i.reshape(1, dr),
      lam.reshape(1, dr))


QK_NOPE = 128
QK_ROPE = 64
V_HEAD = 128
QK_HEAD = QK_NOPE + QK_ROPE


def _rot_half(w):
    half = QK_ROPE // 2
    return jnp.concatenate([-w[..., half:], w[..., :half]], axis=-1)


def mla_weights(w_uq, w_ukv, q_norm, k_norm):
    r, H, _ = w_uq.shape
    half = QK_ROPE // 2
    pair = lambda w: w.reshape(w.shape[0], H // 2, 2 * w.shape[2])
    rope_w = w_uq[..., QK_NOPE:]
    wq = jnp.concatenate([pair(w_uq[..., :QK_NOPE]), pair(rope_w), pair(_rot_half(rope_w))], axis=-1)
    wkv = jnp.concatenate([pair(w_ukv[..., :QK_NOPE]), pair(w_ukv[..., QK_NOPE:])], axis=-1)
    wq = wq.transpose(1, 0, 2).astype(BF16)
    wkv = wkv.transpose(1, 0, 2).astype(BF16)

    def gains(g):
        gr = g[QK_NOPE:]
        swapped = jnp.concatenate([gr[half:], gr[:half]])
        return jnp.stack([g[:QK_NOPE], jnp.tile(gr, 2), jnp.tile(swapped, 2)]).astype(F32)

    inv_freq = ROPE_THETA ** (-jnp.arange(0, QK_ROPE, 2, dtype=F32) / QK_ROPE)
    freq = jnp.tile(inv_freq, 2 * LANES // QK_ROPE).reshape(1, LANES)

    rows = jnp.arange(3 * LANES)[:, None]
    cols = jnp.arange(2 * LANES)[None, :]
    head_b = cols >= LANES
    in_nope = (rows // LANES) == jnp.where(head_b, 1, 0)
    in_rope = (rows >= 2 * LANES) & (((rows - 2 * LANES) // QK_ROPE) == jnp.where(head_b, 1, 0))
    seg_q = (in_nope | in_rope).astype(BF16)
    seg_k = in_nope[:2 * LANES].astype(BF16)
    return wq, wkv, gains(q_norm), gains(k_norm), freq, seg_q, seg_k


def _mla_prep_kernel(cq_ref, ckv_ref, kre_ref, pos_ref, qan_ref, kvan_ref, wq_ref, wkv_ref,
                     gq_ref, gk_ref, freq_ref, segq_ref, segk_ref, q_ref, k_ref, vt_ref, cqn, ckvn):
    tm = cq_ref.shape[0]
    n_pairs = wq_ref.shape[0]

    def latent_norm(ref, g_ref):
        c = ref[...].astype(F32)
        ms = jnp.mean(c * c, axis=-1, keepdims=True)
        return (c * lax.rsqrt(ms + EPS) * g_ref[...]).astype(BF16)

    cqn[...] = latent_norm(cq_ref, qan_ref)
    ckvn[...] = latent_norm(ckv_ref, kvan_ref)

    lane = lax.broadcasted_iota(jnp.int32, (tm, LANES), 1)
    lo = lane < QK_ROPE
    ang = pos_ref[0].astype(F32) * freq_ref[...]
    cos_t, sin_t = jnp.cos(ang), jnp.sin(ang)
    q_cos = gq_ref[1:2, :] * cos_t
    q_sin = gq_ref[2:3, :] * sin_t
    q_scale = (QK_HEAD ** -0.5) * 1.4426950408889634

    kre = kre_ref[...].astype(F32)
    ss_kr = jnp.dot(jnp.where(lo, kre * kre, 0.0).astype(BF16), segk_ref[0:LANES, 0:LANES],
                    preferred_element_type=F32)
    z = kre * jnp.where(lo, gk_ref[1:2, :] * cos_t, gk_ref[2:3, :] * sin_t)
    k_rope = z + pltpu.roll(z, QK_ROPE, 1)
    k_rope_a = jnp.where(lo, k_rope, 0.0)
    k_rope_b = jnp.where(lo, 0.0, k_rope)

    def seg_sums(pieces, seg_ref):
        sq = jnp.concatenate([(p * p).astype(BF16) for p in pieces], axis=1)
        return jnp.dot(sq, seg_ref[...], preferred_element_type=F32)

    def pair(p, carry):
        qp = jnp.dot(cqn[...], wq_ref[p], preferred_element_type=F32)
        qa, qb, rope_raw, rot_raw = (qp[:, j * LANES:(j + 1) * LANES] for j in range(4))
        ss = seg_sums([qa, qb, rope_raw], segq_ref)
        inv = lax.rsqrt(ss * (1.0 / QK_HEAD) + EPS) * q_scale
        rope = rope_raw * q_cos + rot_raw * q_sin
        for j, qn in enumerate((qa, qb)):
            inv_j = inv[:, j * LANES:(j + 1) * LANES]
            q_ref[0, 2 * p + j] = jnp.concatenate(
                [qn * gq_ref[0:1, :] * inv_j, rope * inv_j], axis=1).astype(q_ref.dtype)

        kvp = jnp.dot(ckvn[...], wkv_ref[p], preferred_element_type=F32)
        ka, kb, va, vb = (kvp[:, j * LANES:(j + 1) * LANES] for j in range(4))
        ssk = seg_sums([ka, kb], segk_ref)
        for j, (kn, kr_j, vj) in enumerate(((ka, k_rope_a, va), (kb, k_rope_b, vb))):
            inv_j = lax.rsqrt((ssk[:, j * LANES:(j + 1) * LANES] + ss_kr) * (1.0 / QK_HEAD) + EPS)
            k_ref[0, 2 * p + j] = jnp.concatenate(
                [kn * gk_ref[0:1, :] * inv_j, kr_j * inv_j], axis=1).astype(k_ref.dtype)
            vt_ref[0, 2 * p + j] = vj.T.astype(vt_ref.dtype)
        return carry

    lax.fori_loop(0, n_pairs, pair, 0)


def mla_prep(proj, positions, q_a_norm, kv_a_norm, wq, wkv, gq, gk, freq, seg_q, seg_k, batch, seq, tm,
             cq_blk, ckv_blk, kre_blk):
    n_pairs, q_lora, _ = wq.shape
    kv_lora = wkv.shape[1]
    H = 2 * n_pairs
    per_b = seq // tm
    dq = QK_NOPE + LANES
    row = lambda blk: (lambda b, i: (b * per_b + i, blk))
    const2 = lambda b, i: (0, 0)
    const3 = lambda b, i: (0, 0, 0)
    out_map = lambda b, i: (b, 0, i, 0)
    return pl.pallas_call(
        _mla_prep_kernel,
        grid=(batch, per_b),
        in_specs=[pl.BlockSpec((tm, q_lora), row(cq_blk)),
                  pl.BlockSpec((tm, kv_lora), row(ckv_blk)),
                  pl.BlockSpec((tm, LANES), row(kre_blk)),
                  pl.BlockSpec((1, tm, 1), lambda b, i: (b, i, 0)),
                  pl.BlockSpec((1, q_lora), const2),
                  pl.BlockSpec((1, kv_lora), const2),
                  pl.BlockSpec(wq.shape, const3),
                  pl.BlockSpec(wkv.shape, const3),
                  pl.BlockSpec(gq.shape, const2),
                  pl.BlockSpec(gk.shape, const2),
                  pl.BlockSpec((1, LANES), const2),
                  pl.BlockSpec(seg_q.shape, const2),
                  pl.BlockSpec(seg_k.shape, const2)],
        out_specs=[pl.BlockSpec((1, H, tm, dq), out_map),
                   pl.BlockSpec((1, H, tm, dq), out_map),
                   pl.BlockSpec((1, H, V_HEAD, tm), lambda b, i: (b, 0, 0, i))],
        out_shape=[jax.ShapeDtypeStruct((batch, H, seq, dq), BF16),
                   jax.ShapeDtypeStruct((batch, H, seq, dq), BF16),
                   jax.ShapeDtypeStruct((batch, H, V_HEAD, seq), BF16)],
        scratch_shapes=[pltpu.VMEM((tm, q_lora), BF16), pltpu.VMEM((tm, kv_lora), BF16)],
        compiler_params=_cparams(("arbitrary", "arbitrary")),
        name="mla_prep",
    )(proj, proj, proj, positions.reshape(batch, seq, 1), q_a_norm.reshape(1, q_lora),
      kv_a_norm.reshape(1, kv_lora), wq, wkv, gq, gk, freq, seg_q, seg_k)


def _attn_kernel(q_ref, k_ref, vt_ref, o_ref, qt_ref, s0_ref, s1_ref, m_ref, l_ref, acc_ref):
    nh, tq = q_ref.shape[1], q_ref.shape[2]
    dv = vt_ref.shape[2]
    i = pl.program_id(2)
    m_ref[...] = jnp.full(m_ref.shape, -jnp.inf, F32)
    l_ref[...] = jnp.zeros(l_ref.shape, F32)
    acc_ref[...] = jnp.zeros(acc_ref.shape, F32)
    for h in range(nh):
        qt_ref[h] = q_ref[0, h].astype(F32).T.astype(qt_ref.dtype)

    def scores(j, s_ref):
        start = pl.multiple_of(j * tq, tq)
        for h in range(nh):
            s_ref[h] = jnp.dot(k_ref[0, h, pl.ds(start, tq), :], qt_ref[h], preferred_element_type=F32)

    def consume(j, s_ref, masked):
        start = pl.multiple_of(j * tq, tq)
        for h in range(nh):
            s = s_ref[h]
            vt = vt_ref[0, h, :, pl.ds(start, tq)]
            if masked:
                kc = lax.broadcasted_iota(jnp.int32, s.shape, 0) // CHUNK
                qc = lax.broadcasted_iota(jnp.int32, s.shape, 1) // CHUNK
                s = jnp.where(kc <= qc, s, -jnp.inf)
            m_prev = m_ref[h]
            m_new = jnp.maximum(m_prev, jnp.max(s, axis=0, keepdims=True))
            alpha = jnp.exp2(m_prev - m_new)
            p = jnp.exp2(s - m_new)
            l_ref[h] = alpha * l_ref[h] + jnp.sum(p, axis=0, keepdims=True)
            acc_ref[h] = alpha * acc_ref[h] + jnp.dot(vt, p.astype(vt.dtype), preferred_element_type=F32)
            m_ref[h] = m_new

    scores(0, s0_ref)

    def two_blocks(jj, carry):
        scores(2 * jj + 1, s1_ref)
        consume(2 * jj, s0_ref, False)
        scores(2 * jj + 2, s0_ref)
        consume(2 * jj + 1, s1_ref, False)
        return carry

    lax.fori_loop(0, i // 2, two_blocks, 0)

    @pl.when(i % 2 == 0)
    def _():
        consume(i, s0_ref, True)

    @pl.when(i % 2 == 1)
    def _():
        scores(i, s1_ref)
        consume(i - 1, s0_ref, False)
        consume(i, s1_ref, True)

    for h in range(nh):
        o_ref[0, :, h * dv:(h + 1) * dv] = (acc_ref[h] / l_ref[h]).T.astype(o_ref.dtype)


def mla_attention(q, k, vt, tq, nh):
    B, H, S, dq = q.shape
    dv = vt.shape[2]
    assert tq % CHUNK == 0 and S % tq == 0 and H % nh == 0
    head_map = lambda b, g, i: (b, g, 0, 0)
    return pl.pallas_call(
        _attn_kernel,
        grid=(B, H // nh, S // tq),
        in_specs=[pl.BlockSpec((1, nh, tq, dq), lambda b, g, i: (b, g, i, 0)),
                  pl.BlockSpec((1, nh, S, dq), head_map, pipeline_mode=pl.Buffered(1)),
                  pl.BlockSpec((1, nh, dv, S), head_map, pipeline_mode=pl.Buffered(1))],
        out_specs=pl.BlockSpec((1, tq, nh * dv), lambda b, g, i: (b, i, g)),
        out_shape=jax.ShapeDtypeStruct((B, S, H * dv), BF16),
        scratch_shapes=[pltpu.VMEM((nh, dq, tq), BF16),
                        pltpu.VMEM((nh, tq, tq), F32), pltpu.VMEM((nh, tq, tq), F32),
                        pltpu.VMEM((nh, 1, tq), F32), pltpu.VMEM((nh, 1, tq), F32),
                        pltpu.VMEM((nh, dv, tq), F32)],
        compiler_params=_cparams(("arbitrary", "arbitrary", "arbitrary")),
        name="mla_attention",
    )(q, k, vt)


def _merge_route_kernel(yr_ref, ym_ref, gr_ref, gm_ref, x_ref, mod_ref, wr_ref, wm_ref, wo_ref,
                        n2_ref, wrt_ref, rb_ref,
                        x1_ref, v2_ref, sel_e_ref, sel_r_ref, sel_w_ref, cnt_ref, cnt_acc, logit_buf,
                        *, n_experts):
    tm = x_ref.shape[0]
    step = pl.program_id(0)

    @pl.when(step == 0)
    def _():
        cnt_acc[...] = jnp.zeros_like(cnt_acc)
        logit_buf[...] = jnp.zeros_like(logit_buf)

    a = jnp.dot(yr_ref[...], wr_ref[...], preferred_element_type=F32)
    b = jnp.dot(ym_ref[...], wm_ref[...], preferred_element_type=F32)
    merged = (jax.nn.sigmoid(gr_ref[...].astype(F32)) * a
              + jax.nn.sigmoid(gm_ref[...].astype(F32)) * b)
    c = jnp.dot(merged.astype(BF16), wo_ref[...], preferred_element_type=F32)
    x1 = x_ref[...] + mod_ref[0, 2:3, :] * c
    x1_ref[...] = x1

    ms = jnp.mean(x1 * x1, axis=-1, keepdims=True)
    v = x1 * lax.rsqrt(ms + EPS) * n2_ref[...]
    v = v * (1.0 + mod_ref[0, 4:5, :]) + mod_ref[0, 3:4, :]
    _store_tokens(v2_ref, v)

    v_hi = v.astype(BF16)
    v_lo = (v - v_hi.astype(F32)).astype(BF16)
    logits = (jnp.dot(v_hi, wrt_ref[0], preferred_element_type=F32)
              + jnp.dot(v_hi, wrt_ref[1], preferred_element_type=F32)
              + jnp.dot(v_lo, wrt_ref[0], preferred_element_type=F32))
    prev_logits = logit_buf[...]
    logit_buf[...] = logits

    scores = jax.nn.sigmoid(prev_logits)
    lane = lax.broadcasted_iota(jnp.int32, (tm, LANES), 1)
    biased = jnp.where(lane < n_experts, scores + rb_ref[...], -jnp.inf)
    picked = jnp.zeros((tm, LANES), jnp.bool_)
    idxs, tops = [], []
    for _ in range(TOP_K):
        mx = jnp.max(biased, axis=-1, keepdims=True)
        idx = jnp.min(jnp.where(biased == mx, lane, LANES), axis=-1, keepdims=True)
        hit = lane == idx
        idxs.append(idx)
        tops.append(jnp.sum(jnp.where(hit, scores, 0.0), axis=-1, keepdims=True))
        biased = jnp.where(hit, -jnp.inf, biased)
        picked = jnp.logical_or(picked, hit)
    denom = tops[0]
    for t in tops[1:]:
        denom = denom + t

    onehot = jnp.where(picked, 1.0, 0.0)
    r_i = lax.broadcasted_iota(jnp.int32, (tm, tm), 0)
    c_i = lax.broadcasted_iota(jnp.int32, (tm, tm), 1)
    tri = jnp.where(c_i < r_i, 1.0, 0.0).astype(BF16)
    rank = jnp.dot(tri, onehot.astype(BF16), preferred_element_type=F32) + cnt_acc[0:1, :]
    counted = jnp.where(step > 0, 1.0, 0.0)
    cnt_acc[0:1, :] = cnt_acc[0:1, :] + counted * jnp.sum(onehot, axis=0, keepdims=True)
    cnt_ref[...] = cnt_acc[...]

    sel_e = jnp.zeros((tm, LANES), jnp.int32)
    sel_r = jnp.zeros((tm, LANES), jnp.int32)
    sel_w = jnp.zeros((tm, LANES), F32)
    for kk in range(TOP_K):
        here = lane == kk
        rk = jnp.sum(jnp.where(lane == idxs[kk], rank, 0.0), axis=-1, keepdims=True)
        sel_e = jnp.where(here, idxs[kk], sel_e)
        sel_r = jnp.where(here, rk.astype(jnp.int32), sel_r)
        sel_w = jnp.where(here, tops[kk] / denom * ROUTED_SCALE, sel_w)
    sel_e_ref[...] = sel_e
    sel_r_ref[...] = sel_r
    sel_w_ref[...] = sel_w


def merge_and_route(y_rnn, y_mla, proj, gr_blk, gm_blk, x2, mod, w_rnn_out, w_mla_out, w_out, norm2,
                    w_router_pad, router_bias_pad, n_experts, seq, tm):
    T, D = x2.shape
    per_b = seq // tm
    token_rows = D // (2 * LANES)
    n_tiles = T // tm
    tile = lambda i: jnp.minimum(i, n_tiles - 1)
    row = lambda blk: (lambda i: (tile(i), blk))
    const = lambda i: (0, 0)
    resident = lambda shape: pl.BlockSpec(shape, const, pipeline_mode=pl.Buffered(1))
    lanes_out = lambda: pl.BlockSpec((tm, LANES), lambda i: (jnp.maximum(i - 1, 0), 0))
    return pl.pallas_call(
        functools.partial(_merge_route_kernel, n_experts=n_experts),
        grid=(n_tiles + 1,),
        in_specs=[pl.BlockSpec((tm, D), row(0)),
                  pl.BlockSpec((tm, D), row(0)),
                  pl.BlockSpec((tm, D), row(gr_blk)),
                  pl.BlockSpec((tm, D), row(gm_blk)),
                  pl.BlockSpec((tm, D), row(0)),
                  pl.BlockSpec((1,) + mod.shape[1:], lambda i: (tile(i) // per_b, 0, 0)),
                  resident((D, D)), resident((D, D)), resident((D, D)),
                  pl.BlockSpec((1, D), const),
                  pl.BlockSpec((2, D, LANES), lambda i: (0, 0, 0)),
                  pl.BlockSpec((1, LANES), const)],
        out_specs=[pl.BlockSpec((tm, D), row(0)),
                   pl.BlockSpec((tm * token_rows, LANES), row(0)),
                   lanes_out(), lanes_out(), lanes_out(),
                   pl.BlockSpec((SUBLANES, LANES), const)],
        out_shape=[jax.ShapeDtypeStruct((T, D), F32),
                   jax.ShapeDtypeStruct((T * token_rows, LANES), jnp.uint32),
                   jax.ShapeDtypeStruct((T, LANES), jnp.int32),
                   jax.ShapeDtypeStruct((T, LANES), jnp.int32),
                   jax.ShapeDtypeStruct((T, LANES), F32),
                   jax.ShapeDtypeStruct((SUBLANES, LANES), F32)],
        scratch_shapes=[pltpu.VMEM((SUBLANES, LANES), F32), pltpu.VMEM((tm, LANES), F32)],
        compiler_params=_cparams(("arbitrary",)),
        name="merge_and_route",
    )(y_rnn, y_mla, proj, proj, x2, mod, w_rnn_out, w_mla_out, w_out, norm2.reshape(1, D),
      w_router_pad, router_bias_pad)


def _pack_pair(lo, hi):
    return pltpu.pack_elementwise([lo, hi], packed_dtype=BF16)


def _unpack_pair(p):
    lo = pltpu.unpack_elementwise(p, index=0, packed_dtype=BF16, unpacked_dtype=F32)
    hi = pltpu.unpack_elementwise(p, index=1, packed_dtype=BF16, unpacked_dtype=F32)
    return lo, hi


def _packed_zeros(shape):
    z = jnp.zeros(shape, F32)
    return _pack_pair(z, z)


def _store_tokens(ref, x):
    n, d = x.shape
    half = d // 2
    rows = half // LANES
    for c in range(rows):
        lo = x[:, c * LANES:(c + 1) * LANES]
        hi = x[:, half + c * LANES:half + (c + 1) * LANES]
        ref[pl.ds(c, n, stride=rows), :] = _pack_pair(lo, hi)


def _load_tokens(ref, n):
    rows = ref.shape[0] // n
    los, his = [], []
    for c in range(rows):
        lo, hi = _unpack_pair(ref[pl.ds(c, n, stride=rows), :])
        los.append(lo)
        his.append(hi)
    return jnp.concatenate(los, axis=1), jnp.concatenate(his, axis=1)


def _load_tokens_bf16(ref, n):
    lo, hi = _load_tokens(ref, n)
    return jnp.concatenate([lo.astype(BF16), hi.astype(BF16)], axis=1)


def _swiglu(x, wg, wu, wd):
    g = jnp.dot(x, wg, preferred_element_type=F32)
    u = jnp.dot(x, wu, preferred_element_type=F32)
    h = (g * jax.nn.sigmoid(g)) * u
    return jnp.dot(h.astype(BF16), wd, preferred_element_type=F32)


def _token_window(ref, tok, rows, count=1):
    return ref.at[pl.ds(pl.multiple_of(tok * rows, rows), count * rows), :]


def _token_copy(src, src_tok, dst, dst_tok, sem, rows):
    return pltpu.make_async_copy(_token_window(src, src_tok, rows), _token_window(dst, dst_tok, rows), sem)


def _dispatch_kernel(dest_ref, fill_ref, v_ref, wg_ref, wu_ref, wd_ref, xs_hbm, sh_ref, zeros, sem, zsem,
                     *, n_experts, rows):
    tm = v_ref.shape[0] // rows
    n_copies = tm * TOP_K

    for r in range(tm):
        for kk in range(TOP_K):
            _token_copy(v_ref, r, xs_hbm, dest_ref[r * DEST_SLOTS + kk], sem, rows).start(priority=kk % 2)
    shared = _swiglu(_load_tokens_bf16(v_ref, tm), wg_ref[...], wu_ref[...], wd_ref[...])
    sh_ref[...] = shared.astype(sh_ref.dtype)

    @pl.when(pl.program_id(0) == 0)
    def _():
        zeros[...] = _packed_zeros(zeros.shape)

        def per_expert(e, c):
            first, count = fill_ref[e], fill_ref[n_experts + e]

            def one(r, c2):
                _token_copy(zeros, 0, xs_hbm, first + r, zsem, rows).start()
                return c2

            def one_wait(r, c2):
                _token_copy(zeros, 0, xs_hbm, first, zsem, rows).wait()
                return c2

            lax.fori_loop(0, count, one, 0)
            lax.fori_loop(0, count, one_wait, 0)
            return c

        lax.fori_loop(0, n_experts, per_expert, 0)

        def block_copy(b):
            return pltpu.make_async_copy(zeros, _token_window(xs_hbm, b * MOE_BLOCK, rows, MOE_BLOCK), zsem)

        def tail(b, c):
            block_copy(b).start()
            block_copy(b).wait()
            return c

        lax.fori_loop(fill_ref[2 * n_experts], xs_hbm.shape[0] // (MOE_BLOCK * rows), tail, 0)

    done = _token_window(xs_hbm, 0, rows, n_copies)
    pltpu.make_async_copy(done, done, sem).wait()


def moe_dispatch_shared(v2t, dest_flat, fill, ws_gate, ws_up, ws_down, n_slots, n_experts, rows, tm):
    T = v2t.shape[0] // rows
    D, de = ws_gate.shape
    const = lambda i: (0, 0)
    resident = lambda shape: pl.BlockSpec(shape, const, pipeline_mode=pl.Buffered(1))
    return pl.pallas_call(
        functools.partial(_dispatch_kernel, n_experts=n_experts, rows=rows),
        grid=(T // tm,),
        in_specs=[pl.BlockSpec((tm * DEST_SLOTS,), lambda i: (i,), memory_space=pltpu.SMEM),
                  pl.BlockSpec(memory_space=pltpu.SMEM),
                  pl.BlockSpec((tm * rows, LANES), lambda i: (i, 0)),
                  resident((D, de)), resident((D, de)), resident((de, D))],
        out_specs=[pl.BlockSpec(memory_space=pl.ANY), pl.BlockSpec((tm, D), lambda i: (i, 0))],
        out_shape=[jax.ShapeDtypeStruct((n_slots * rows, LANES), jnp.uint32),
                   jax.ShapeDtypeStruct((T, D), BF16)],
        scratch_shapes=[pltpu.VMEM((MOE_BLOCK * rows, LANES), jnp.uint32),
                        pltpu.SemaphoreType.DMA(()), pltpu.SemaphoreType.DMA(())],
        compiler_params=_cparams(("arbitrary",)),
        name="moe_dispatch_shared",
    )(dest_flat, fill, v2t, ws_gate, ws_up, ws_down)


WEIGHT_SLAB = 128
WEIGHT_STAGES = 4


def _expert_ffn_kernel(sched_ref, nused_ref, xs_ref, wg_hbm, wu_hbm, wd_hbm, ys_ref,
                       wg_buf, wu_buf, wd_buf, stage, sems):
    i = pl.program_id(0)
    d_model, de = wg_buf.shape[1], wg_buf.shape[2]
    n_gu = d_model // WEIGHT_SLAB
    n_slabs = 2 * n_gu + de // WEIGHT_SLAB
    nb = pl.num_programs(0)
    expert, slot, nxt = sched_ref[i], sched_ref[nb + i], sched_ref[2 * nb + i]
    lo, hi, used = sched_ref[3 * nb + i], sched_ref[4 * nb + i], sched_ref[5 * nb + i]

    def slab_copy(e, s, j):
        def desc(w_hbm, k, width):
            r0 = pl.multiple_of((s - k) * WEIGHT_SLAB, WEIGHT_SLAB)
            return pltpu.make_async_copy(w_hbm.at[e, pl.ds(r0, WEIGHT_SLAB), :],
                                         stage.at[j, :, pl.ds(0, width)], sems.at[j])
        return (desc(wg_hbm, 0, de), desc(wu_hbm, n_gu, de), desc(wd_hbm, 2 * n_gu, d_model))

    def by_kind(s, fns):
        pl.when(s < n_gu)(fns[0])
        pl.when(jnp.logical_and(s >= n_gu, s < 2 * n_gu))(fns[1])
        pl.when(s >= 2 * n_gu)(fns[2])

    def start_slabs(e, first, last):
        def body(s, c):
            by_kind(s, [d.start for d in slab_copy(e, s, s - first)])
            return c
        lax.fori_loop(first, last, body, 0)

    def finish_slabs(e, first, last, dst_slot):
        def body(s, c):
            j = s - first
            descs = slab_copy(e, s, j)

            def land(d, buf, k, width):
                def f():
                    d.wait()
                    r0 = pl.multiple_of((s - k) * WEIGHT_SLAB, WEIGHT_SLAB)
                    buf[dst_slot, pl.ds(r0, WEIGHT_SLAB), :] = stage[j, :, 0:width].astype(BF16)
                return f

            by_kind(s, [land(descs[0], wg_buf, 0, de), land(descs[1], wu_buf, n_gu, de),
                        land(descs[2], wd_buf, 2 * n_gu, d_model)])
            return c
        lax.fori_loop(first, last, body, 0)

    def fetch_rounds(e, first, last, dst_slot):
        def body(r, c):
            a = first + r * WEIGHT_STAGES
            b = jnp.minimum(a + WEIGHT_STAGES, last)
            start_slabs(e, a, b)
            finish_slabs(e, a, b, dst_slot)
            return c
        lax.fori_loop(0, (last - first + WEIGHT_STAGES - 1) // WEIGHT_STAGES, body, 0)

    @pl.when(i == 0)
    def _():
        fetch_rounds(expert, 0, n_slabs, slot)

    mid = jnp.minimum(lo + WEIGHT_STAGES, hi)
    start_slabs(nxt, lo, mid)

    @pl.when(used == 1)
    def _():
        x = _load_tokens_bf16(xs_ref, MOE_BLOCK)
        _store_tokens(ys_ref, _swiglu(x, wg_buf[slot], wu_buf[slot], wd_buf[slot]))

    @pl.when(used == 0)
    def _():
        ys_ref[...] = _packed_zeros(ys_ref.shape)

    finish_slabs(nxt, lo, mid, 1 - slot)
    fetch_rounds(nxt, mid, hi, 1 - slot)


def _expert_schedule(blk_e, n_used, counts, n_slabs):
    nb = blk_e.shape[0]
    n_experts = counts.shape[0]
    e_ids = jnp.arange(n_experts, dtype=jnp.int32)
    has = counts > 0
    blocks = (counts + MOE_BLOCK - 1) // MOE_BLOCK
    first_blk = jnp.cumsum(blocks) - blocks
    order = jnp.cumsum(has.astype(jnp.int32)) - 1
    later = jnp.where(has, e_ids, n_experts)
    nxt = jnp.concatenate([lax.cummin(later[::-1])[::-1][1:], jnp.full((1,), n_experts, jnp.int32)])
    blk = jnp.arange(nb, dtype=jnp.int32)
    used = blk < n_used
    pick = lambda table: jnp.sum(jnp.where(blk_e[:, None] == e_ids[None, :], table[None, :], 0), axis=1)
    nb_e, j = pick(blocks), blk - pick(first_blk)
    nxt_b = pick(nxt)
    per = (n_slabs + jnp.maximum(nb_e, 1) - 1) // jnp.maximum(nb_e, 1)
    fetch = used & (nxt_b < n_experts)
    lo = jnp.where(fetch, jnp.minimum(j * per, n_slabs), 0)
    hi = jnp.where(fetch, jnp.minimum((j + 1) * per, n_slabs), 0)
    return jnp.stack([blk_e, pick(order) % 2, jnp.minimum(nxt_b, n_experts - 1), lo, hi,
                      used.astype(jnp.int32)]).astype(jnp.int32)


def moe_expert_ffn(xs, blk_e, n_used, counts, w_gate, w_up, w_down, rows):
    blk_rows = MOE_BLOCK * rows
    nb = xs.shape[0] // blk_rows
    E, D, de = w_gate.shape
    assert D % WEIGHT_SLAB == 0 and de % WEIGHT_SLAB == 0
    sched = _expert_schedule(blk_e, n_used[0], counts, 2 * (D // WEIGHT_SLAB) + de // WEIGHT_SLAB)
    last = lambda i, s, nu: (jnp.minimum(i, nu[0] - 1), 0)
    hbm = pl.BlockSpec(memory_space=pl.ANY)
    return pl.pallas_call(
        _expert_ffn_kernel,
        grid_spec=pltpu.PrefetchScalarGridSpec(
            num_scalar_prefetch=2,
            grid=(nb,),
            in_specs=[pl.BlockSpec((blk_rows, LANES), last), hbm, hbm, hbm],
            out_specs=pl.BlockSpec((blk_rows, LANES), lambda i, s, nu: (i, 0)),
            scratch_shapes=[pltpu.VMEM((2, D, de), BF16), pltpu.VMEM((2, D, de), BF16),
                            pltpu.VMEM((2, de, D), BF16),
                            pltpu.VMEM((WEIGHT_STAGES, WEIGHT_SLAB, max(D, de)), F32),
                            pltpu.SemaphoreType.DMA((WEIGHT_STAGES,))]),
        out_shape=jax.ShapeDtypeStruct(xs.shape, jnp.uint32),
        compiler_params=_cparams(("arbitrary",)),
        name="moe_expert_ffn",
    )(sched.reshape(-1), n_used, xs, w_gate, w_up, w_down)


def _combine_kernel(dest_ref, dest_next_ref, sh_ref, x1_ref, w_ref, mod_ref, ys_hbm, o_ref, ybuf, sems,
                    *, rows):
    tm = x1_ref.shape[0]
    dh = x1_ref.shape[1] // 2
    n_copies = tm * TOP_K
    step = pl.program_id(0)
    slot = step % 2

    def gather(d_ref, s, inline):
        def start(r, c=0):
            for kk in range(TOP_K):
                _token_copy(ys_hbm, d_ref[r * DEST_SLOTS + kk], ybuf.at[s, kk], r, sems.at[s],
                            rows).start(priority=kk % 2)
            return c
        if inline:
            for r in range(tm):
                start(r)
        else:
            lax.fori_loop(0, tm, start, 0, unroll=2)

    done = _token_window(ys_hbm, 0, rows, n_copies)

    @pl.when(step == 0)
    def _():
        gather(dest_ref, 0, False)

    pltpu.make_async_copy(done, done, sems.at[slot]).wait()
    lo, hi = sh_ref[:, :dh].astype(F32), sh_ref[:, dh:].astype(F32)
    for kk in range(TOP_K):
        w = w_ref[:, kk:kk + 1]
        ylo, yhi = _load_tokens(ybuf.at[slot, kk], tm)
        lo = lo + w * ylo
        hi = hi + w * yhi
    g2 = mod_ref[0, 5:6, :]
    out_lo = x1_ref[:, :dh] + g2[:, :dh] * lo
    out_hi = x1_ref[:, dh:] + g2[:, dh:] * hi

    gather(dest_next_ref, 1 - slot, True)
    o_ref[:, :dh] = out_lo
    o_ref[:, dh:] = out_hi

    @pl.when(step == pl.num_programs(0) - 1)
    def _():
        pltpu.make_async_copy(done, done, sems.at[1 - slot]).wait()


def moe_combine(shared, x1, sel_w, dest_flat, ys, mod, seq, rows, tm):
    T, D = x1.shape
    per_b = seq // tm
    return pl.pallas_call(
        functools.partial(_combine_kernel, rows=rows),
        grid=(T // tm,),
        in_specs=[pl.BlockSpec((tm * DEST_SLOTS,), lambda i: (i,), memory_space=pltpu.SMEM),
                  pl.BlockSpec((tm * DEST_SLOTS,), lambda i: (jnp.minimum(i + 1, T // tm - 1),),
                               memory_space=pltpu.SMEM),
                  pl.BlockSpec((tm, D), lambda i: (i, 0)),
                  pl.BlockSpec((tm, D), lambda i: (i, 0)),
                  pl.BlockSpec((tm, LANES), lambda i: (i, 0)),
                  pl.BlockSpec((1,) + mod.shape[1:], lambda i: (i // per_b, 0, 0)),
                  pl.BlockSpec(memory_space=pl.ANY)],
        out_specs=pl.BlockSpec((tm, D), lambda i: (i, 0)),
        out_shape=jax.ShapeDtypeStruct((T, D), F32),
        scratch_shapes=[pltpu.VMEM((2, TOP_K, tm * rows, LANES), jnp.uint32), pltpu.SemaphoreType.DMA((2,))],
        compiler_params=_cparams(("arbitrary",)),
        name="moe_combine",
    )(dest_flat, dest_flat, shared, x1, sel_w, mod, ys)


def _in_proj_layout(w_in, d_rnn, q_lora, kv_lora, d_model):
    o = [0]
    for wdt in (d_rnn, d_rnn, q_lora, kv_lora, QK_ROPE, d_model, d_model):
        o.append(o[-1] + wdt)
    xr, yg, cq, ckv, kr, g_rnn, g_mla = (w_in[:, o[i]:o[i + 1]] for i in range(7))
    pieces = [xr, yg, g_rnn, g_mla, cq, ckv, jnp.concatenate([kr, _rot_half(kr)], axis=1)]
    starts, pos = [], 0
    for p in pieces:
        assert pos % p.shape[1] == 0, "column block of the input projection is not aligned to its width"
        starts.append(pos // p.shape[1])
        pos += p.shape[1]
    n_pad = -(-pos // 512) * 512
    pieces.append(jnp.zeros((w_in.shape[0], n_pad - pos), w_in.dtype))
    return jnp.concatenate(pieces, axis=1).astype(BF16), starts


def _routing_tables(counts, sel_e, sel_r, n_tokens, n_experts):
    counts = counts.astype(jnp.int32)
    padded = (counts + MOE_BLOCK - 1) // MOE_BLOCK * MOE_BLOCK
    pad_end = jnp.cumsum(padded)
    pad_start = pad_end - padded
    nb = -(-(n_tokens * TOP_K) // MOE_BLOCK) + n_experts
    n_used = pad_end[-1] // MOE_BLOCK
    blk = jnp.arange(nb, dtype=jnp.int32)
    blk_e = jnp.sum((pad_end[None, :] <= (blk * MOE_BLOCK)[:, None]).astype(jnp.int32), axis=1)
    blk_e = jnp.minimum(blk_e, n_experts - 1)
    last_e = jnp.sum(jnp.where(blk == n_used - 1, blk_e, 0))
    blk_e = jnp.where(blk < n_used, blk_e, last_e)
    e_ids = jnp.arange(n_experts, dtype=jnp.int32)
    se, sr = sel_e[:, :DEST_SLOTS], sel_r[:, :DEST_SLOTS]
    dest = sr + jnp.sum(jnp.where(se[:, :, None] == e_ids, pad_start, 0), axis=-1)
    fill = jnp.concatenate([pad_start + counts, padded - counts, n_used.reshape(1)])
    return dest.reshape(-1), fill, blk_e, n_used.reshape(1), nb * MOE_BLOCK


def _layer(x2, c, positions, batch, seq, w_mod, b_mod, norm1, w_in, conv_w, conv_b, w_a, b_a, w_i, b_i,
           lru_lambda, q_a_norm, kv_a_norm, w_uq, w_ukv, q_norm, k_norm, w_rnn_out, w_mla_out, w_out,
           norm2, w_router, router_bias, w_gate, w_up, w_down, ws_gate, ws_up, ws_down):
    T, D = x2.shape
    d_rnn = conv_w.shape[1]
    q_lora, kv_lora = q_a_norm.shape[0], kv_a_norm.shape[0]
    n_experts = w_router.shape[1]
    assert w_uq.shape[2] == QK_HEAD and w_ukv.shape[2] == QK_NOPE + V_HEAD
    assert d_rnn == D and w_mla_out.shape[0] == D and n_experts <= LANES

    mod = adaln_mod(c, w_mod, b_mod)

    w_cat, (_, _, gr_blk, gm_blk, cq_blk, ckv_blk, kre_blk) = _in_proj_layout(w_in, d_rnn, q_lora, kv_lora, D)
    proj = in_projection(x2, mod, norm1, w_cat, seq, _tile(seq, 256, 16))

    wai = jnp.concatenate([w_a, w_i], axis=-1).astype(BF16)
    y_rnn = rglru_branch(proj, conv_w, conv_b, wai, b_a, b_i, lru_lambda, batch, seq, _tile(seq, 256, 16))

    wq, wkv, gq, gk, freq, seg_q, seg_k = mla_weights(w_uq, w_ukv, q_norm, k_norm)
    q, k, vt = mla_prep(proj, positions, q_a_norm, kv_a_norm, wq, wkv, gq, gk, freq, seg_q, seg_k, batch, seq,
                        _tile(seq, 512, LANES), cq_blk, ckv_blk, kre_blk)
    n_heads = w_uq.shape[1]
    y_mla = mla_attention(q, k, vt, _tile(seq, 512, LANES), 4 if n_heads % 4 == 0 else 2).reshape(T, -1)

    w_router_pad = jnp.zeros((D, LANES), F32).at[:, :n_experts].set(w_router)
    w_router_hi = w_router_pad.astype(BF16)
    w_router_pad = jnp.stack([w_router_hi, (w_router_pad - w_router_hi.astype(F32)).astype(BF16)])
    router_bias_pad = jnp.zeros((1, LANES), F32).at[0, :n_experts].set(router_bias)
    x1, v2t, sel_e, sel_r, sel_w, cnt = merge_and_route(
        y_rnn, y_mla, proj, gr_blk, gm_blk, x2, mod, w_rnn_out.astype(BF16), w_mla_out.astype(BF16),
        w_out.astype(BF16), norm2, w_router_pad, router_bias_pad, n_experts, seq, _tile(seq, 256, 16))

    token_rows = D // (2 * LANES)
    dest, fill, blk_e, n_used, n_slots = _routing_tables(cnt[0, :n_experts], sel_e, sel_r, T, n_experts)
    tm_moe = _tile(seq, 256, 128)
    xs, shared = moe_dispatch_shared(v2t, dest, fill, ws_gate.astype(BF16), ws_up.astype(BF16),
                                     ws_down.astype(BF16), n_slots, n_experts, token_rows, tm_moe)
    ys = moe_expert_ffn(xs, blk_e, n_used, cnt[0, :n_experts].astype(jnp.int32), w_gate, w_up, w_down,
                        token_rows)
    return moe_combine(shared, x1, sel_w, dest, ys, mod, seq, token_rows, tm_moe)


def kernel(x, c, positions, w_mod, b_mod, norm1, w_in, conv_w, conv_b, w_a, b_a, w_i, b_i, lru_lambda,
           q_a_norm, kv_a_norm, w_uq, w_ukv, q_norm, k_norm, w_rnn_out, w_mla_out, w_out, norm2, w_router,
           router_bias, w_gate, w_up, w_down, ws_gate, ws_up, ws_down):
    B, S, D = x.shape
    x2 = x.reshape(B * S, D)
    per_layer = (w_mod, b_mod, norm1, w_in, conv_w, conv_b, w_a, b_a, w_i, b_i, lru_lambda, q_a_norm,
                 kv_a_norm, w_uq, w_ukv, q_norm, k_norm, w_rnn_out, w_mla_out, w_out, norm2, w_router,
                 router_bias, w_gate, w_up, w_down, ws_gate, ws_up, ws_down)
    for l in range(w_mod.shape[0]):
        x2 = _layer(x2, c, positions, B, S, *(w[l] for w in per_layer))
    return x2.reshape(B, S, D)
```
